```python
import math
import jax, jax.numpy as jnp
from jax import lax
import numpy as np

D_MODEL = 1024
BATCH = 8
SEQ = 2048
DEPTH = 4

CHUNK = 64
QBLK = 128
DSA_QBLK = 64
EPS = 1e-6

LRU_WIDTH = D_MODEL // 2
LRU_BLOCKS = 8
LRU_BW = LRU_WIDTH // LRU_BLOCKS
CONV_W = 4
LRU_C = 8.0

MLA_HEADS = 8
MLA_NOPE = 64
MLA_ROPE = 32
MLA_V = 64
MLA_Q_RANK = D_MODEL // 4
MLA_KV_RANK = D_MODEL // 8
ROPE_THETA = 10000.0

EVEN_SPLITS = (LRU_WIDTH, LRU_WIDTH, MLA_Q_RANK, MLA_KV_RANK, MLA_ROPE)
EVEN_IN = sum(EVEN_SPLITS)
EVEN_MIX = LRU_WIDTH + MLA_HEADS * MLA_V

DSA_HEADS = 16
DSA_KV_HEADS = 4
DSA_HD = 64
IDX_HEADS = 8
IDX_HD = 64
TOPK_MAX = 256
ODD_SPLITS = (DSA_HEADS * DSA_HD, DSA_KV_HEADS * DSA_HD, DSA_KV_HEADS * DSA_HD,
              IDX_HEADS * IDX_HD, IDX_HD, IDX_HEADS)
ODD_IN = sum(ODD_SPLITS)
ODD_MIX = DSA_HEADS * DSA_HD

D_FF = 4 * D_MODEL

N_EVEN = (DEPTH + 1) // 2
N_ODD = DEPTH // 2

kernel_name = "hybrid_chunk_causal_rglru_mla_dsa"


def rms_norm(x, g):
    xf = x.astype(jnp.float32)
    y = xf * lax.rsqrt(jnp.mean(xf * xf, axis=-1, keepdims=True) + EPS)
    return (y * g.astype(jnp.float32)).astype(x.dtype)


def split_cols(u, sizes):
    offs = [int(o) for o in np.cumsum(sizes)[:-1]]
    return jnp.split(u, offs, axis=-1)


def rope(x, cos, sin):
    half = x.shape[-1] // 2
    x1, x2 = x[..., :half], x[..., half:]
    return jnp.concatenate([x1 * cos - x2 * sin, x1 * sin + x2 * cos], axis=-1)


def causal_conv(x, w, b):
    S = x.shape[1]
    xp = jnp.pad(x, ((0, 0), (CONV_W - 1, 0), (0, 0)))
    return sum(xp[:, j:j + S] * w[j] for j in range(CONV_W)) + b


def rg_lru(x, ga_w, ga_b, gx_w, gx_b, lam):
    B, S, W = x.shape
    xf = x.astype(jnp.float32)
    xb = xf.reshape(B, S, LRU_BLOCKS, LRU_BW)
    r = jax.nn.sigmoid(jnp.einsum('bsnk,nkj->bsnj', xb, ga_w.astype(jnp.float32)).reshape(B, S, W)
                       + ga_b.astype(jnp.float32))
    i = jax.nn.sigmoid(jnp.einsum('bsnk,nkj->bsnj', xb, gx_w.astype(jnp.float32)).reshape(B, S, W)
                       + gx_b.astype(jnp.float32))
    log_a = -LRU_C * r * jax.nn.softplus(-lam.astype(jnp.float32))
    a = jnp.exp(log_a)
    mult = jnp.sqrt(-jnp.expm1(2.0 * log_a))
    bterm = mult * (i * xf)

    def combine(left, right):
        a_l, b_l = left
        a_r, b_r = right
        return a_l * a_r, a_r * b_l + b_r

    _, h = lax.associative_scan(combine, (a, bterm), axis=1)
    return h.astype(x.dtype)


def mla_attention(q_nope, q_rope, k_nope, k_rope, v):
    S = q_nope.shape[1]
    scale = (MLA_NOPE + MLA_ROPE) ** -0.5
    outs = []
    for blk in range(S // QBLK):
        q0, q1 = blk * QBLK, (blk + 1) * QBLK
        s = (jnp.einsum('bqhd,bkhd->bhqk', q_nope[:, q0:q1], k_nope[:, :q1])
             + jnp.einsum('bqhr,bkr->bhqk', q_rope[:, q0:q1], k_rope[:, :q1]))
        s = s.astype(jnp.float32) * scale
        qi = jnp.arange(q0, q1) // CHUNK
        ki = jnp.arange(q1) // CHUNK
        mask = ki[None, :] <= qi[:, None]
        s = jnp.where(mask, s, -jnp.inf)
        p = jax.nn.softmax(s, axis=-1).astype(v.dtype)
        outs.append(jnp.einsum('bhqk,bkhd->bqhd', p, v[:, :q1]))
    return jnp.concatenate(outs, axis=1)


def dsa_attention(q, k, v, q_idx, k_idx, w_idx):
    B, S = q.shape[0], q.shape[1]
    topk = min(TOPK_MAX, S // 4)
    rep = DSA_HEADS // DSA_KV_HEADS
    scale = DSA_HD ** -0.5
    idx_scale = IDX_HD ** -0.5
    bidx = jnp.arange(B)[:, None, None]
    outs = []
    for blk in range(S // DSA_QBLK):
        q0, q1 = blk * DSA_QBLK, (blk + 1) * DSA_QBLK
        kl = min(S, max(q1, topk))
        qi = jnp.arange(q0, q1) // CHUNK
        ki = jnp.arange(kl) // CHUNK
        adm = ki[None, :] <= qi[:, None]
        logits = jnp.einsum('bqhd,bkd->bqhk', q_idx[:, q0:q1], k_idx[:, :kl]).astype(jnp.float32) * idx_scale
        score = jnp.einsum('bqh,bqhk->bqk', w_idx[:, q0:q1].astype(jnp.float32), jax.nn.relu(logits))
        score = jnp.where(adm[None], score, -jnp.inf)
        _, sel = lax.top_k(score, topk)
        valid = (sel // CHUNK) <= qi[None, :, None]
        ks = k[bidx, sel]
        vs = v[bidx, sel]
        qb = q[:, q0:q1].reshape(B, DSA_QBLK, DSA_KV_HEADS, rep, DSA_HD)
        s = jnp.einsum('bqgrd,bqkgd->bgrqk', qb, ks).astype(jnp.float32) * scale
        s = jnp.where(valid[:, None, None], s, -jnp.inf)
        p = jax.nn.softmax(s, axis=-1).astype(v.dtype)
        o = jnp.einsum('bgrqk,bqkgd->bqgrd', p, vs)
        outs.append(o.reshape(B, DSA_QBLK, DSA_HEADS * DSA_HD))
    return jnp.concatenate(outs, axis=1)


def even_mixer(x, cos, sin, norm_g, w_in, conv_w, conv_b, ga_w, ga_b, gx_w, gx_b, lam,
               q_norm, w_uq, kv_norm, w_ukv, w_out):
    B, S, _ = x.shape
    h = rms_norm(x, norm_g)
    u = h @ w_in
    xr, gr, cq, ckv, kr = split_cols(u, EVEN_SPLITS)
    hr = rg_lru(causal_conv(xr, conv_w, conv_b), ga_w, ga_b, gx_w, gx_b, lam)
    y_lru = hr * jax.nn.gelu(gr)
    q = (rms_norm(cq, q_norm) @ w_uq).reshape(B, S, MLA_HEADS, MLA_NOPE + MLA_ROPE)
    q_nope, q_rope = q[..., :MLA_NOPE], q[..., MLA_NOPE:]
    kv = (rms_norm(ckv, kv_norm) @ w_ukv).reshape(B, S, MLA_HEADS, MLA_NOPE + MLA_V)
    k_nope, v = kv[..., :MLA_NOPE], kv[..., MLA_NOPE:]
    q_rope = rope(q_rope, cos[:, :, None, :], sin[:, :, None, :])
    k_rope = rope(kr, cos, sin)
    y_mla = mla_attention(q_nope, q_rope, k_nope, k_rope, v).reshape(B, S, MLA_HEADS * MLA_V)
    return jnp.concatenate([y_lru, y_mla], axis=-1) @ w_out


def odd_mixer(x, norm_g, w_in, idx_k_norm, w_out):
    B, S, _ = x.shape
    h = rms_norm(x, norm_g)
    u = h @ w_in
    q, k, v, qi, ki, wi = split_cols(u, ODD_SPLITS)
    q = q.reshape(B, S, DSA_HEADS, DSA_HD)
    k = k.reshape(B, S, DSA_KV_HEADS, DSA_HD)
    v = v.reshape(B, S, DSA_KV_HEADS, DSA_HD)
    qi = qi.reshape(B, S, IDX_HEADS, IDX_HD)
    ki = rms_norm(ki, idx_k_norm)
    wi = wi * (IDX_HEADS ** -0.5)
    return dsa_attention(q, k, v, qi, ki, wi) @ w_out


def mlp(x, norm_g, w1, w2):
    h = rms_norm(x, norm_g)
    return jnp.square(jax.nn.relu(h @ w1)) @ w2


def setup_inputs(seed: int = 0) -> dict:
    key = jax.random.key(seed)
    ks = iter(jax.random.split(key, 32))
    f32 = jnp.float32

    def nrm(shape, fan_in):
        return jax.random.normal(next(ks), shape, f32) * (fan_in ** -0.5)

    def gain(shape):
        return 1.0 + 0.02 * jax.random.normal(next(ks), shape, f32)

    def bias(shape):
        return 0.02 * jax.random.normal(next(ks), shape, f32)

    x = jax.random.normal(next(ks), (BATCH, SEQ, D_MODEL), f32)
    positions = jnp.broadcast_to(jnp.arange(SEQ, dtype=jnp.int32), (BATCH, SEQ))
    a0 = jax.random.uniform(next(ks), (N_EVEN, LRU_WIDTH), f32, 0.9, 0.999)
    p = a0 ** (1.0 / LRU_C)
    lam = jnp.log(p) - jnp.log1p(-p)
    return {
        "x": x,
        "positions": positions,
        "e_norm": gain((N_EVEN, D_MODEL)),
        "e_w_in": nrm((N_EVEN, D_MODEL, EVEN_IN), D_MODEL),
        "e_conv_w": nrm((N_EVEN, CONV_W, LRU_WIDTH), CONV_W),
        "e_conv_b": bias((N_EVEN, LRU_WIDTH)),
        "e_ga_w": nrm((N_EVEN, LRU_BLOCKS, LRU_BW, LRU_BW), LRU_BW),
        "e_ga_b": bias((N_EVEN, LRU_WIDTH)),
        "e_gx_w": nrm((N_EVEN, LRU_BLOCKS, LRU_BW, LRU_BW), LRU_BW),
        "e_gx_b": bias((N_EVEN, LRU_WIDTH)),
        "e_lambda": lam,
        "e_q_norm": gain((N_EVEN, MLA_Q_RANK)),
        "e_w_uq": nrm((N_EVEN, MLA_Q_RANK, MLA_HEADS * (MLA_NOPE + MLA_ROPE)), MLA_Q_RANK),
        "e_kv_norm": gain((N_EVEN, MLA_KV_RANK)),
        "e_w_ukv": nrm((N_EVEN, MLA_KV_RANK, MLA_HEADS * (MLA_NOPE + MLA_V)), MLA_KV_RANK),
        "e_w_out": nrm((N_EVEN, EVEN_MIX, D_MODEL), EVEN_MIX),
        "o_norm": gain((N_ODD, D_MODEL)),
        "o_w_in": nrm((N_ODD, D_MODEL, ODD_IN), D_MODEL),
        "o_idx_k_norm": gain((N_ODD, IDX_HD)),
        "o_w_out": nrm((N_ODD, ODD_MIX, D_MODEL), ODD_MIX),
        "m_norm": gain((DEPTH, D_MODEL)),
        "m_w1": nrm((DEPTH, D_MODEL, D_FF), D_MODEL),
        "m_w2": nrm((DEPTH, D_FF, D_MODEL), D_FF),
        "final_norm": gain((D_MODEL,)),
    }


def reference(x, positions, e_norm, e_w_in, e_conv_w, e_conv_b, e_ga_w, e_ga_b, e_gx_w, e_gx_b,
              e_lambda, e_q_norm, e_w_uq, e_kv_norm, e_w_ukv, e_w_out, o_norm, o_w_in,
              o_idx_k_norm, o_w_out, m_norm, m_w1, m_w2, final_norm):
    freqs = ROPE_THETA ** (-jnp.arange(0, MLA_ROPE, 2, dtype=jnp.float32) / MLA_ROPE)
    ang = positions.astype(jnp.float32)[..., None] * freqs
    cos = jnp.cos(ang).astype(x.dtype)
    sin = jnp.sin(ang).astype(x.dtype)
    for l in range(DEPTH):
        j = l // 2
        if l % 2 == 0:
            x = x + even_mixer(x, cos, sin, e_norm[j], e_w_in[j], e_conv_w[j], e_conv_b[j],
                               e_ga_w[j], e_ga_b[j], e_gx_w[j], e_gx_b[j], e_lambda[j],
                               e_q_norm[j], e_w_uq[j], e_kv_norm[j], e_w_ukv[j], e_w_out[j])
        else:
            x = x + odd_mixer(x, o_norm[j], o_w_in[j], o_idx_k_norm[j], o_w_out[j])
        x = x + mlp(x, m_norm[l], m_w1[l], m_w2[l])
    return rms_norm(x, final_norm)
```

```python
import functools
import math

import jax
import jax.numpy as jnp
from jax import lax
from jax.experimental import pallas as pl
from jax.experimental.pallas import tpu as pltpu

F32 = jnp.float32
BF16 = jnp.bfloat16
I32 = jnp.int32

D_MODEL = 1024
DEPTH = 4
CHUNK = 64
EPS = 1e-6

LRU_WIDTH = D_MODEL // 2
LRU_BLOCKS = 8
CONV_W = 4
LRU_C = 8.0

MLA_HEADS = 8
MLA_NOPE = 64
MLA_ROPE = 32
MLA_V = 64
MLA_Q_RANK = D_MODEL // 4
MLA_KV_RANK = D_MODEL // 8
ROPE_THETA = 10000.0

DSA_HEADS = 16
DSA_KV_HEADS = 4
DSA_HD = 64
IDX_HEADS = 8
IDX_HD = 64
TOPK_MAX = 256

D_FF = 4 * D_MODEL

LANES = 128
VMEM_LIMIT = 56 * 1024 * 1024

TM = 512
TQ = 256
LRU_T = 256
MLP_TM = 1024
MLP_TF = 512

EVEN_N = 1664
ODD_N = 3968


def _params(sem):
    return pltpu.CompilerParams(dimension_semantics=sem, vmem_limit_bytes=VMEM_LIMIT)


def _rms(x, g):
    ms = jnp.mean(x * x, axis=-1, keepdims=True)
    return (x * lax.rsqrt(ms + EPS)) * g


def _dot(a, b):
    return jnp.dot(a, b, preferred_element_type=F32)


def _dot_nt(a, b):
    return lax.dot_general(a, b, (((1,), (1,)), ((), ())), preferred_element_type=F32)


def _even_in_kernel(x_ref, g_ref, w_ref, qn_ref, wq_ref, kvn_ref, wkv_ref, tq_ref, tc_ref, ts_ref,
                    vone_ref, xr_ref, gr_ref, q_ref, k_ref, v_ref):
    h = _rms(x_ref[...], g_ref[...]).astype(BF16)
    xr_ref[...] = _dot(h, w_ref[:, 0:512])
    gr_ref[...] = _dot(h, w_ref[:, 512:1024])
    cq = _dot(h, w_ref[:, 1024:1280])
    ckv = _dot(h, w_ref[:, 1280:1408])
    kr = _dot(h, w_ref[:, 1408:1536])
    krs = _dot(h, w_ref[:, 1536:1664])
    krope = kr * tc_ref[...] + krs * ts_ref[...]
    cqn = _rms(cq, qn_ref[...]).astype(BF16)
    ckvn = _rms(ckv, kvn_ref[...]).astype(BF16)
    tq = tq_ref[...]
    for hd in range(MLA_HEADS):
        sl = slice(LANES * hd, LANES * (hd + 1))
        q_ref[:, sl] = (_dot(cqn, wq_ref[:, sl]) * tq).astype(BF16)
        k_ref[:, sl] = (_dot(ckvn, wkv_ref[:, sl]) + krope).astype(BF16)
        vs = slice(1024 + LANES * hd, 1024 + LANES * (hd + 1))
        v_ref[:, sl] = (_dot(ckvn, wkv_ref[:, vs]) + vone_ref[:, sl]).astype(BF16)


def _even_in(x2, g, w, qn, wq, kvn, wkv, tq, tc, ts, vone):
    m = x2.shape[0]
    row = lambda n: pl.BlockSpec((TM, n), lambda i: (i, 0))
    full = lambda a: pl.BlockSpec(a.shape, lambda i: (0, 0))
    return pl.pallas_call(
        _even_in_kernel,
        grid=(m // TM,),
        in_specs=[row(D_MODEL), full(g), full(w), full(qn), full(wq), full(kvn), full(wkv),
                  row(LANES), row(LANES), row(LANES), full(vone)],
        out_specs=[row(512), row(512), row(1024), row(1024), row(1024)],
        out_shape=[jax.ShapeDtypeStruct((m, 512), F32), jax.ShapeDtypeStruct((m, 512), F32),
                   jax.ShapeDtypeStruct((m, 1024), BF16), jax.ShapeDtypeStruct((m, 1024), BF16),
                   jax.ShapeDtypeStruct((m, 1024), BF16)],
        compiler_params=_params(("parallel",)),
        name="even_in",
    )(x2, g, w, qn, wq, kvn, wkv, tq, tc, ts, vone)


def _gelu_tanh(x):
    c = math.sqrt(2.0 / math.pi)
    return 0.5 * x * (1.0 + jnp.tanh(c * (x + 0.044715 * (x * x * x))))


def _lru_kernel(xr_ref, gr_ref, cw_ref, cb_ref, wa_ref, ba_ref, wx_ref, bx_ref, cl_ref, y_ref,
                xbuf, hc):
    t = LRU_T

    @pl.when(pl.program_id(1) == 0)
    def _():
        xbuf[0:8, :] = jnp.zeros((8, LRU_WIDTH), F32)
        hc[...] = jnp.zeros((1, LRU_WIDTH), F32)

    x = xr_ref[...]
    xbuf[8:8 + t, :] = x
    xc = cb_ref[...] + cw_ref[3:4, :] * x
    for j in range(CONV_W - 1):
        xc = xc + cw_ref[j:j + 1, :] * xbuf[5 + j:5 + j + t, :]
    xbuf[0:8, :] = x[t - 8:t, :]

    xcb = xc.astype(BF16)
    r = jax.nn.sigmoid(_dot(xcb, wa_ref[...]) + ba_ref[...])
    i = jax.nn.sigmoid(_dot(xcb, wx_ref[...]) + bx_ref[...])
    log_a = cl_ref[...] * r
    a = jnp.exp(log_a)
    th = jnp.tanh(log_a)
    b = jnp.sqrt((-2.0 * th) / (1.0 - th)) * (i * xc)

    rows = lax.broadcasted_iota(I32, (t, LRU_WIDTH), 0)
    d = 1
    while d < t:
        a_sh = pltpu.roll(a, d, 0)
        b_sh = pltpu.roll(b, d, 0)
        keep = rows >= d
        b = jnp.where(keep, a * b_sh + b, b)
        a = jnp.where(keep, a * a_sh, a)
        d *= 2
    h = a * hc[...] + b
    hc[...] = h[t - 1:t, :]
    y_ref[...] = (h * _gelu_tanh(gr_ref[...])).astype(BF16)


def _lru(xr, gr, cw, cb, wa, ba, wx, bx, cl):
    b, s, w = xr.shape
    blk = pl.BlockSpec((None, LRU_T, w), lambda i, j: (i, j, 0))
    full = lambda a: pl.BlockSpec(a.shape, lambda i, j: (0, 0))
    return pl.pallas_call(
        _lru_kernel,
        grid=(b, s // LRU_T),
        in_specs=[blk, blk, full(cw), full(cb), full(wa), full(ba), full(wx), full(bx), full(cl)],
        out_specs=blk,
        out_shape=jax.ShapeDtypeStruct((b, s, w), BF16),
        scratch_shapes=[pltpu.VMEM((LRU_T + 8, w), F32), pltpu.VMEM((1, w), F32)],
        compiler_params=_params(("parallel", "arbitrary")),
        name="rg_lru",
    )(xr, gr, cw, cb, wa, ba, wx, bx, cl)


def _pair_store(o_ref, col, acc_e, acc_o):
    lane = lax.broadcasted_iota(I32, acc_e.shape, 1)
    oe = acc_e * (1.0 / acc_e[:, 64:65])
    oo = acc_o * (1.0 / acc_o[:, 0:1])
    o_ref[:, col:col + LANES] = jnp.where(lane < 64, oe, oo).astype(BF16)


def _mla_block(q_ref, k_ref, v_ref, o_ref, c):
    past = c * TQ
    rc = lax.broadcasted_iota(I32, (TQ, TQ), 0) // CHUNK
    cc = lax.broadcasted_iota(I32, (TQ, TQ), 1) // CHUNK
    dbias = jnp.where(cc <= rc, 0.0, -jnp.inf).astype(F32)
    for p in range(MLA_HEADS // 2):
        accs = []
        for e in range(2):
            sl = slice(LANES * (2 * p + e), LANES * (2 * p + e + 1))
            q = q_ref[:, sl]
            sd = _dot_nt(q, k_ref[past:past + TQ, sl]) + dbias
            m = jnp.max(sd, axis=1, keepdims=True)
            if past:
                sp = _dot_nt(q, k_ref[0:past, sl])
                m = jnp.maximum(m, jnp.max(sp, axis=1, keepdims=True))
            acc = _dot(jnp.exp(sd - m).astype(BF16), v_ref[past:past + TQ, sl])
            if past:
                acc = acc + _dot(jnp.exp(sp - m).astype(BF16), v_ref[0:past, sl])
            accs.append(acc)
        _pair_store(o_ref, LANES * p, accs[0], accs[1])


def _mla_kernel(q_ref, k_ref, v_ref, o_ref):
    c = pl.program_id(1)
    for cc in range(k_ref.shape[0] // TQ):
        pl.when(c == cc)(functools.partial(_mla_block, q_ref, k_ref, v_ref, o_ref, cc))


def _mla(q, k, v):
    b, s, _ = q.shape
    return pl.pallas_call(
        _mla_kernel,
        grid=(b, s // TQ),
        in_specs=[pl.BlockSpec((None, TQ, 1024), lambda i, j: (i, j, 0)),
                  pl.BlockSpec((None, s, 1024), lambda i, j: (i, 0, 0)),
                  pl.BlockSpec((None, s, 1024), lambda i, j: (i, 0, 0))],
        out_specs=pl.BlockSpec((None, TQ, 512), lambda i, j: (i, j, 0)),
        out_shape=jax.ShapeDtypeStruct((b, s, 512), BF16),
        compiler_params=_params(("parallel", "arbitrary")),
        name="mla_attn",
    )(q, k, v)


def _odd_in_kernel(x_ref, g_ref, w_ref, kig_ref, vone_ref, q_ref, k_ref, v_ref, qi_ref, ki_ref, wi_ref):
    h = _rms(x_ref[...], g_ref[...]).astype(BF16)
    for j in range(8):
        sl = slice(LANES * j, LANES * (j + 1))
        q_ref[:, sl] = _dot(h, w_ref[:, sl]).astype(BF16)
        k_ref[:, sl] = _dot(h, w_ref[:, 1024 + LANES * j:1024 + LANES * (j + 1)]).astype(BF16)
        v_ref[:, sl] = (_dot(h, w_ref[:, 2048 + LANES * j:2048 + LANES * (j + 1)])
                        + vone_ref[:, sl]).astype(BF16)
    qi_ref[...] = _dot(h, w_ref[:, 3072:3584]).astype(BF16)
    kia = _dot(h, w_ref[:, 3584:3712])
    kib = _dot(h, w_ref[:, 3712:3840])
    inv = lax.rsqrt(jnp.sum(kia * kia, axis=-1, keepdims=True) * (1.0 / IDX_HD) + EPS)
    ki_ref[:, 0:LANES] = ((kia * inv) * kig_ref[:, 0:LANES]).astype(BF16)
    ki_ref[:, LANES:2 * LANES] = ((kib * inv) * kig_ref[:, LANES:2 * LANES]).astype(BF16)
    wi_ref[...] = (_dot(h, w_ref[:, 3840:3968]) * (IDX_HEADS ** -0.5)) * (IDX_HD ** -0.5)


def _odd_in(x2, g, w, kig, vone):
    m = x2.shape[0]
    row = lambda n: pl.BlockSpec((TM, n), lambda i: (i, 0))
    full = lambda a: pl.BlockSpec(a.shape, lambda i: (0, 0))
    sds = jax.ShapeDtypeStruct
    return pl.pallas_call(
        _odd_in_kernel,
        grid=(m // TM,),
        in_specs=[row(D_MODEL), full(g), full(w), full(kig), full(vone)],
        out_specs=[row(1024), row(1024), row(1024), row(512), row(256), row(LANES)],
        out_shape=[sds((m, 1024), BF16), sds((m, 1024), BF16), sds((m, 1024), BF16),
                   sds((m, 512), BF16), sds((m, 256), BF16), sds((m, LANES), F32)],
        compiler_params=_params(("parallel",)),
        name="odd_in",
    )(x2, g, w, kig, vone)


def _dsa_block(q_ref, k_ref, v_ref, qi_ref, ki_ref, wi_ref, o_ref, key_scr, bias_scr, c, topk):
    kl = TQ * (c + 1)
    past = kl - TQ
    wi = wi_ref[...]

    score = jnp.zeros((TQ, kl), F32)
    for p in range(IDX_HEADS // 2):
        slab = qi_ref[:, LANES * p:LANES * (p + 1)]
        for e in range(2):
            hd = 2 * p + e
            z = _dot_nt(slab, ki_ref[0:kl, LANES * e:LANES * (e + 1)])
            score = score + wi[:, hd:hd + 1] * jnp.maximum(z, 0.0)

    rc = (lax.broadcasted_iota(I32, (TQ, TQ), 0) + past) // CHUNK
    cc = (lax.broadcasted_iota(I32, (TQ, TQ), 1) + past) // CHUNK
    adm_d = cc <= rc

    def to_key(sc):
        bits = pltpu.bitcast(sc, I32)
        key = jnp.where(bits < 0, bits ^ jnp.int32(0x7FFFFFFF), bits)
        return jnp.where(sc == 0.0, 0, key)

    if past:
        key_scr[:, 0:past] = to_key(score[:, 0:past])
    key_scr[:, past:kl] = to_key(jnp.where(adm_d, score[:, past:kl], -jnp.inf))

    def step(i, t):
        cand = t + jnp.left_shift(jnp.int32(1), 31 - i)
        cnt = jnp.sum(jnp.where(key_scr[:, 0:kl] >= cand, 1.0, 0.0), axis=1, keepdims=True)
        return jnp.where(cnt >= float(topk), cand, t)

    t = lax.fori_loop(0, 32, step, jnp.full((TQ, 1), -2 ** 31, I32))

    cgt = jnp.sum(jnp.where(key_scr[:, 0:kl] > t, 1.0, 0.0), axis=1, keepdims=True)
    need = float(topk) - cgt
    ri = lax.broadcasted_iota(I32, (LANES, LANES), 0)
    ci = lax.broadcasted_iota(I32, (LANES, LANES), 1)
    before = jnp.where(ri < ci, 1.0, 0.0).astype(BF16)
    carry = jnp.zeros((TQ, 1), F32)
    for j in range(kl // LANES):
        kj = key_scr[:, LANES * j:LANES * (j + 1)]
        eq = jnp.where(kj == t, 1.0, 0.0)
        prefix = _dot(eq.astype(BF16), before) + carry
        sel = jnp.where(kj > t, 1.0, jnp.where(prefix < need, eq, 0.0))
        if LANES * j >= past:
            off = LANES * j - past
            sel = jnp.where(adm_d[:, off:off + LANES], sel, 0.0)
        bias_scr[:, LANES * j:LANES * (j + 1)] = jnp.where(sel > 0.0, 0.0, -jnp.inf)
        carry = carry + jnp.sum(eq, axis=1, keepdims=True)

    for gp in range(DSA_HEADS // 2):
        g = gp // 2
        slab = q_ref[:, LANES * gp:LANES * (gp + 1)]
        accs = []
        for e in range(2):
            col = 512 * e + LANES * g
            s = _dot_nt(slab, k_ref[0:kl, col:col + LANES]) + bias_scr[:, 0:kl]
            m = jnp.max(s, axis=1, keepdims=True)
            accs.append(_dot(jnp.exp(s - m).astype(BF16), v_ref[0:kl, col:col + LANES]))
        _pair_store(o_ref, LANES * gp, accs[0], accs[1])


def _dsa_kernel(q_ref, k_ref, v_ref, qi_ref, ki_ref, wi_ref, o_ref, key_scr, bias_scr, *, topk):
    c = pl.program_id(1)
    for cc in range(k_ref.shape[0] // TQ):
        pl.when(c == cc)(functools.partial(_dsa_block, q_ref, k_ref, v_ref, qi_ref, ki_ref, wi_ref,
                                           o_ref, key_scr, bias_scr, cc, topk))


def _dsa(q, k, v, qi, ki, wi):
    b, s, _ = q.shape
    topk = min(TOPK_MAX, s // 4)
    assert topk <= TQ and s % TQ == 0
    qblk = lambda n: pl.BlockSpec((None, TQ, n), lambda i, j: (i, j, 0))
    kblk = lambda n: pl.BlockSpec((None, s, n), lambda i, j: (i, 0, 0))
    return pl.pallas_call(
        functools.partial(_dsa_kernel, topk=topk),
        grid=(b, s // TQ),
        in_specs=[qblk(1024), kblk(1024), kblk(1024), qblk(512), kblk(256), qblk(LANES)],
        out_specs=qblk(1024),
        out_shape=jax.ShapeDtypeStruct((b, s, 1024), BF16),
        scratch_shapes=[pltpu.VMEM((TQ, s), I32), pltpu.VMEM((TQ, s), F32)],
        compiler_params=_params(("parallel", "arbitrary")),
        name="dsa_attn",
    )(q, k, v, qi, ki, wi)


def _out_kernel(*refs):
    x_ref, o_ref = refs[0], refs[-1]
    acc = x_ref[...]
    for a_ref, w_ref in zip(refs[1:-1:2], refs[2:-1:2]):
        acc = acc + _dot(a_ref[...], w_ref[...])
    o_ref[...] = acc


def _out_proj(x2, pairs):
    m = x2.shape[0]
    row = lambda n: pl.BlockSpec((TM, n), lambda i: (i, 0))
    full = lambda a: pl.BlockSpec(a.shape, lambda i: (0, 0))
    specs, args = [row(D_MODEL)], [x2]
    for a, w in pairs:
        specs += [row(a.shape[1]), full(w)]
        args += [a, w]
    return pl.pallas_call(
        _out_kernel,
        grid=(m // TM,),
        in_specs=specs,
        out_specs=row(D_MODEL),
        out_shape=jax.ShapeDtypeStruct((m, D_MODEL), F32),
        compiler_params=_params(("parallel",)),
        name="out_proj",
    )(*args)


def _mlp_kernel(x_ref, g_ref, w1_ref, w2_ref, fg_ref, o_ref, h_scr, acc, *, final):
    j = pl.program_id(1)

    @pl.when(j == 0)
    def _():
        h_scr[...] = _rms(x_ref[...], g_ref[...]).astype(BF16)
        acc[...] = jnp.zeros_like(acc)

    a = jnp.maximum(_dot(h_scr[...], w1_ref[...]), 0.0)
    acc[...] += _dot((a * a).astype(BF16), w2_ref[...])

    @pl.when(j == pl.num_programs(1) - 1)
    def _():
        y = x_ref[...] + acc[...]
        o_ref[...] = _rms(y, fg_ref[...]) if final else y


def _mlp(x2, g, w1, w2, fg, final):
    m = x2.shape[0]
    return pl.pallas_call(
        functools.partial(_mlp_kernel, final=final),
        grid=(m // MLP_TM, D_FF // MLP_TF),
        in_specs=[pl.BlockSpec((MLP_TM, D_MODEL), lambda i, j: (i, 0)),
                  pl.BlockSpec((1, D_MODEL), lambda i, j: (0, 0)),
                  pl.BlockSpec((D_MODEL, MLP_TF), lambda i, j: (0, j)),
                  pl.BlockSpec((MLP_TF, D_MODEL), lambda i, j: (j, 0)),
                  pl.BlockSpec((1, D_MODEL), lambda i, j: (0, 0))],
        out_specs=pl.BlockSpec((MLP_TM, D_MODEL), lambda i, j: (i, 0)),
        out_shape=jax.ShapeDtypeStruct((m, D_MODEL), F32),
        scratch_shapes=[pltpu.VMEM((MLP_TM, D_MODEL), BF16), pltpu.VMEM((MLP_TM, D_MODEL), F32)],
        compiler_params=_params(("parallel", "arbitrary")),
        name="mlp",
    )(x2, g, w1, w2, fg)


def _rot_cols(w):
    half = w.shape[-1] // 2
    return jnp.concatenate([-w[..., half:], w[..., :half]], axis=-1)


def _one_hot_row(n, idx):
    return jnp.zeros((1, n), F32).at[0, jnp.asarray(idx)].set(1.0)


def _even_weights(w_in, w_uq, w_ukv):
    d = w_in.shape[0]
    wkr = w_in[:, 1408:1440]
    z64 = jnp.zeros((d, 64), F32)
    w = jnp.concatenate([w_in[:, :1408], z64, wkr, wkr, z64, _rot_cols(wkr), _rot_cols(wkr)], axis=1)
    uq = w_uq.reshape(MLA_Q_RANK, MLA_HEADS, MLA_NOPE + MLA_ROPE)
    rope_w = uq[..., MLA_NOPE:]
    wq = jnp.concatenate([uq[..., :MLA_NOPE], rope_w, _rot_cols(rope_w)], axis=-1)
    wq = wq.reshape(MLA_Q_RANK, MLA_HEADS * LANES)
    ukv = w_ukv.reshape(MLA_KV_RANK, MLA_HEADS, MLA_NOPE + MLA_V)
    zk = jnp.zeros((MLA_KV_RANK, MLA_HEADS, 64), F32)
    wk = jnp.concatenate([ukv[..., :MLA_NOPE], zk], axis=-1).reshape(MLA_KV_RANK, MLA_HEADS * LANES)
    vv = ukv[..., MLA_NOPE:].reshape(MLA_KV_RANK, MLA_HEADS // 2, 2, MLA_V)
    zv = jnp.zeros((MLA_KV_RANK, MLA_HEADS // 2, MLA_V), F32)
    v_even = jnp.concatenate([vv[:, :, 0], zv], axis=-1)
    v_odd = jnp.concatenate([zv, vv[:, :, 1]], axis=-1)
    wv = jnp.stack([v_even, v_odd], axis=2).reshape(MLA_KV_RANK, MLA_HEADS * LANES)
    wkv = jnp.concatenate([wk, wv], axis=1)
    return w.astype(BF16), wq.astype(BF16), wkv.astype(BF16)


def _odd_weights(w_in):
    d = w_in.shape[0]
    wq = w_in[:, :1024] * (DSA_HD ** -0.5)
    wk = w_in[:, 1024:1280].reshape(d, DSA_KV_HEADS, DSA_HD)
    wv = w_in[:, 1280:1536].reshape(d, DSA_KV_HEADS, DSA_HD)
    z = jnp.zeros((d, DSA_KV_HEADS, DSA_HD), F32)
    two = lambda a: jnp.concatenate([jnp.concatenate([a, z], -1).reshape(d, -1),
                                     jnp.concatenate([z, a], -1).reshape(d, -1)], axis=1)
    wqi = w_in[:, 1536:2048]
    wki = w_in[:, 2048:2112]
    z64 = jnp.zeros((d, 64), F32)
    wwi = jnp.concatenate([w_in[:, 2112:2120], jnp.zeros((d, LANES - IDX_HEADS), F32)], axis=1)
    w = jnp.concatenate([wq, two(wk), two(wv), wqi, wki, z64, z64, wki, wwi], axis=1)
    return w.astype(BF16)


def _block_diag(w):
    n, k, _ = w.shape
    eye = jnp.eye(n, dtype=w.dtype)
    return (w[:, :, None, :] * eye[:, None, :, None]).reshape(n * k, n * k)


def kernel(x, positions, e_norm, e_w_in, e_conv_w, e_conv_b, e_ga_w, e_ga_b, e_gx_w, e_gx_b, e_lambda,
           e_q_norm, e_w_uq, e_kv_norm, e_w_ukv, e_w_out, o_norm, o_w_in, o_idx_k_norm, o_w_out,
           m_norm, m_w1, m_w2, final_norm):
    b, s, d = x.shape
    m = b * s
    row = lambda v: v.reshape(1, -1).astype(F32)

    freqs = ROPE_THETA ** (-jnp.arange(0, MLA_ROPE, 2, dtype=F32) / MLA_ROPE)
    ang = positions.astype(F32)[..., None] * freqs
    cos2 = jnp.tile(jnp.cos(ang), (1, 1, 2)).reshape(m, MLA_ROPE)
    sin2 = jnp.tile(jnp.sin(ang), (1, 1, 2)).reshape(m, MLA_ROPE)
    z64 = jnp.zeros((m, 64), F32)
    scale = (MLA_NOPE + MLA_ROPE) ** -0.5
    tq = scale * jnp.concatenate([jnp.ones((m, 64), F32), cos2, sin2], axis=1)
    tc = jnp.concatenate([z64, cos2, cos2], axis=1)
    ts = jnp.concatenate([z64, sin2, sin2], axis=1)
    vone_even = _one_hot_row(MLA_HEADS * LANES, [LANES * h + (64 if h % 2 == 0 else 0) for h in range(MLA_HEADS)])
    vone_odd = _one_hot_row(2 * DSA_KV_HEADS * LANES,
                            [LANES * g + 64 for g in range(DSA_KV_HEADS)]
                            + [512 + LANES * g for g in range(DSA_KV_HEADS)])

    x2 = x.reshape(m, d)
    for l in range(DEPTH):
        j = l // 2
        if l % 2 == 0:
            w, wq, wkv = _even_weights(e_w_in[j], e_w_uq[j], e_w_ukv[j])
            xr, gr, q, k, v = _even_in(x2, row(e_norm[j]), w, row(e_q_norm[j]), wq, row(e_kv_norm[j]), wkv,
                                       tq, tc, ts, vone_even)
            cl = -LRU_C * jax.nn.softplus(-e_lambda[j].astype(F32))
            y_lru = _lru(xr.reshape(b, s, -1), gr.reshape(b, s, -1), e_conv_w[j].astype(F32),
                         row(e_conv_b[j]), _block_diag(e_ga_w[j]).astype(BF16), row(e_ga_b[j]),
                         _block_diag(e_gx_w[j]).astype(BF16), row(e_gx_b[j]), row(cl))
            y_mla = _mla(q.reshape(b, s, -1), k.reshape(b, s, -1), v.reshape(b, s, -1))
            wo = e_w_out[j].astype(BF16)
            x2 = _out_proj(x2, [(y_lru.reshape(m, -1), wo[:LRU_WIDTH]), (y_mla.reshape(m, -1), wo[LRU_WIDTH:])])
        else:
            w = _odd_weights(o_w_in[j])
            kig = jnp.concatenate([row(o_idx_k_norm[j]), jnp.zeros((1, 128), F32), row(o_idx_k_norm[j])], axis=1)
            q, k, v, qi, ki, wi = _odd_in(x2, row(o_norm[j]), w, kig, vone_odd)
            r3 = lambda a: a.reshape(b, s, -1)
            y = _dsa(r3(q), r3(k), r3(v), r3(qi), r3(ki), r3(wi))
            x2 = _out_proj(x2, [(y.reshape(m, -1), o_w_out[j].astype(BF16))])
        x2 = _mlp(x2, row(m_norm[l]), m_w1[l].astype(BF16), m_w2[l].astype(BF16), row(final_norm),
                  final=(l == DEPTH - 1))
    return x2.reshape(b, s, d)
```

```python
import functools
import math

import jax
import jax.numpy as jnp
from jax import lax
from jax.experimental import pallas as pl
from jax.experimental.pallas import tpu as pltpu

F32 = jnp.float32
BF16 = jnp.bfloat16
I32 = jnp.int32

D_MODEL = 1024
DEPTH = 4
CHUNK = 64
EPS = 1e-6

LRU_WIDTH = D_MODEL // 2
LRU_BLOCKS = 8
CONV_W = 4
LRU_C = 8.0

MLA_HEADS = 8
MLA_NOPE = 64
MLA_ROPE = 32
MLA_V = 64
MLA_Q_RANK = D_MODEL // 4
MLA_KV_RANK = D_MODEL // 8
ROPE_THETA = 10000.0

DSA_HEADS = 16
DSA_KV_HEADS = 4
DSA_HD = 64
IDX_HEADS = 8
IDX_HD = 64
TOPK_MAX = 256

D_FF = 4 * D_MODEL

LANES = 128
VMEM_LIMIT = 56 * 1024 * 1024
LOG2E = 1.4426950408889634
NEG = -1e30

TM = 512
TQ = 128
TK = 256
LRU_T = 256
MLP_TM = 1024
MLP_TF = 512
NSLAB = 8


def _params(sem):
    return pltpu.CompilerParams(dimension_semantics=sem, vmem_limit_bytes=VMEM_LIMIT)


def _rms(x, g):
    ms = jnp.mean(x * x, axis=-1, keepdims=True)
    return (x * lax.rsqrt(ms + EPS)) * g


def _dot(a, b):
    return jnp.dot(a, b, preferred_element_type=F32)


def _dot_nt(a, b):
    return lax.dot_general(a, b, (((1,), (1,)), ((), ())), preferred_element_type=F32)


def _slab_spec(nslab, rows, seq):
    nt = seq // rows
    return pl.BlockSpec((None, nslab, rows, LANES), lambda i: (i // nt, 0, i % nt, 0))


def _even_in_kernel(x_ref, g_ref, w_ref, qn_ref, wq_ref, kvn_ref, wkv_ref, tq_ref, tc_ref, ts_ref,
                    vone_ref, xr_ref, gr_ref, q_ref, k_ref, v_ref):
    h = _rms(x_ref[...], g_ref[...]).astype(BF16)
    xr_ref[...] = _dot(h, w_ref[:, 0:512])
    gr_ref[...] = _dot(h, w_ref[:, 512:1024])
    cq = _dot(h, w_ref[:, 1024:1280])
    ckv = _dot(h, w_ref[:, 1280:1408])
    kr = _dot(h, w_ref[:, 1408:1536])
    krs = _dot(h, w_ref[:, 1536:1664])
    krope = kr * tc_ref[...] + krs * ts_ref[...]
    cqn = _rms(cq, qn_ref[...]).astype(BF16)
    ckvn = _rms(ckv, kvn_ref[...]).astype(BF16)
    tq = tq_ref[...]
    for hd in range(MLA_HEADS):
        sl = slice(LANES * hd, LANES * (hd + 1))
        q_ref[hd] = (_dot(cqn, wq_ref[:, sl]) * tq).astype(BF16)
        k_ref[hd] = (_dot(ckvn, wkv_ref[:, sl]) + krope).astype(BF16)
        vs = slice(1024 + LANES * hd, 1024 + LANES * (hd + 1))
        v_ref[hd] = (_dot(ckvn, wkv_ref[:, vs]) + vone_ref[:, sl]).astype(BF16)


def _even_in(x2, g, w, qn, wq, kvn, wkv, tq, tc, ts, vone, b, s):
    m = x2.shape[0]
    row = lambda n: pl.BlockSpec((TM, n), lambda i: (i, 0))
    full = lambda a: pl.BlockSpec(a.shape, lambda i: (0, 0))
    slab = _slab_spec(NSLAB, TM, s)
    sds = jax.ShapeDtypeStruct
    return pl.pallas_call(
        _even_in_kernel,
        grid=(m // TM,),
        in_specs=[row(D_MODEL), full(g), full(w), full(qn), full(wq), full(kvn), full(wkv),
                  row(LANES), row(LANES), row(LANES), full(vone)],
        out_specs=[row(512), row(512), slab, slab, slab],
        out_shape=[sds((m, 512), F32), sds((m, 512), F32), sds((b, NSLAB, s, LANES), BF16),
                   sds((b, NSLAB, s, LANES), BF16), sds((b, NSLAB, s, LANES), BF16)],
        compiler_params=_params(("parallel",)),
        name="even_in",
    )(x2, g, w, qn, wq, kvn, wkv, tq, tc, ts, vone)


def _gelu_tanh(x):
    c = math.sqrt(2.0 / math.pi)
    return 0.5 * x * (1.0 + jnp.tanh(c * (x + 0.044715 * (x * x * x))))


def _lru_kernel(xr_ref, gr_ref, cw_ref, cb_ref, wa_ref, ba_ref, wx_ref, bx_ref, cl_ref, y_ref,
                xbuf, hc):
    t = LRU_T

    @pl.when(pl.program_id(1) == 0)
    def _():
        xbuf[0:8, :] = jnp.zeros((8, LRU_WIDTH), F32)
        hc[...] = jnp.zeros((1, LRU_WIDTH), F32)

    x = xr_ref[...]
    xbuf[8:8 + t, :] = x
    xc = cb_ref[...] + cw_ref[3:4, :] * x
    for j in range(CONV_W - 1):
        xc = xc + cw_ref[j:j + 1, :] * xbuf[5 + j:5 + j + t, :]
    xbuf[0:8, :] = x[t - 8:t, :]

    xcb = xc.astype(BF16)
    r = jax.nn.sigmoid(_dot(xcb, wa_ref[...]) + ba_ref[...])
    i = jax.nn.sigmoid(_dot(xcb, wx_ref[...]) + bx_ref[...])
    log_a = cl_ref[...] * r
    a = jnp.exp(log_a)
    th = jnp.tanh(log_a)
    b = jnp.sqrt((-2.0 * th) / (1.0 - th)) * (i * xc)

    rows = lax.broadcasted_iota(I32, (t, LRU_WIDTH), 0)
    d = 1
    while d < t:
        a_sh = pltpu.roll(a, d, 0)
        b_sh = pltpu.roll(b, d, 0)
        keep = rows >= d
        b = jnp.where(keep, a * b_sh + b, b)
        a = jnp.where(keep, a * a_sh, a)
        d *= 2
    h = a * hc[...] + b
    hc[...] = h[t - 1:t, :]
    y_ref[...] = (h * _gelu_tanh(gr_ref[...])).astype(BF16)


def _lru(xr, gr, cw, cb, wa, ba, wx, bx, cl):
    b, s, w = xr.shape
    blk = pl.BlockSpec((None, LRU_T, w), lambda i, j: (i, j, 0))
    full = lambda a: pl.BlockSpec(a.shape, lambda i, j: (0, 0))
    return pl.pallas_call(
        _lru_kernel,
        grid=(b, s // LRU_T),
        in_specs=[blk, blk, full(cw), full(cb), full(wa), full(ba), full(wx), full(bx), full(cl)],
        out_specs=blk,
        out_shape=jax.ShapeDtypeStruct((b, s, w), BF16),
        scratch_shapes=[pltpu.VMEM((LRU_T + 8, w), F32), pltpu.VMEM((1, w), F32)],
        compiler_params=_params(("parallel", "arbitrary")),
        name="rg_lru",
    )(xr, gr, cw, cb, wa, ba, wx, bx, cl)


def _key_rows(kb):
    return pl.ds(pl.multiple_of(kb * TK, TK), TK)


def _chunk_admissible(c, kb):
    shift = CHUNK.bit_length() - 1
    rc = (lax.broadcasted_iota(I32, (TQ, 1), 0) + c * TQ) >> shift
    cc = (lax.broadcasted_iota(I32, (1, TK), 1) + kb * TK) >> shift
    return cc <= rc


def _flash_tile(q, k, v, bias, carry):
    m, acc = carry
    s = _dot_nt(q, k)
    if bias is not None:
        s = s + bias
    m_new = jnp.maximum(m, jnp.max(s, axis=1, keepdims=True))
    p = jnp.exp2(s - m_new).astype(BF16)
    return m_new, jnp.exp2(m - m_new) * acc + _dot(p, v)


def _flash_init():
    return jnp.full((TQ, 1), -jnp.inf, F32), jnp.zeros((TQ, LANES), F32)


def _pair_slab(acc_e, acc_o):
    lane = lax.broadcasted_iota(I32, acc_e.shape, 1)
    oe = acc_e * (1.0 / acc_e[:, 64:65])
    oo = acc_o * (1.0 / acc_o[:, 0:1])
    return jnp.where(lane < 64, oe, oo).astype(BF16)


def _mla_kernel(q_ref, k_ref, v_ref, o_ref):
    c = pl.program_id(1)
    kd = (c * TQ) // TK
    dbias = jnp.where(_chunk_admissible(c, kd), 0.0, NEG).astype(F32)
    for p in range(MLA_HEADS // 2):
        accs = []
        for e in range(2):
            hd = 2 * p + e
            q = q_ref[hd]

            def past_tile(kb, carry, q=q, hd=hd):
                rows = _key_rows(kb)
                return _flash_tile(q, k_ref[hd, rows, :], v_ref[hd, rows, :], None, carry)

            carry = lax.fori_loop(0, kd, past_tile, _flash_init())
            rows = _key_rows(kd)
            _, acc = _flash_tile(q, k_ref[hd, rows, :], v_ref[hd, rows, :], dbias, carry)
            accs.append(acc)
        o_ref[:, LANES * p:LANES * (p + 1)] = _pair_slab(accs[0], accs[1])


def _mla(q, k, v):
    b, _, s, _ = q.shape
    return pl.pallas_call(
        _mla_kernel,
        grid=(b, s // TQ),
        in_specs=[pl.BlockSpec((None, NSLAB, TQ, LANES), lambda i, j: (i, 0, j, 0)),
                  pl.BlockSpec((None, NSLAB, s, LANES), lambda i, j: (i, 0, 0, 0)),
                  pl.BlockSpec((None, NSLAB, s, LANES), lambda i, j: (i, 0, 0, 0))],
        out_specs=pl.BlockSpec((None, TQ, 512), lambda i, j: (i, j, 0)),
        out_shape=jax.ShapeDtypeStruct((b, s, 512), BF16),
        compiler_params=_params(("parallel", "arbitrary")),
        name="mla_attn",
    )(q, k, v)


def _odd_in_kernel(x_ref, g_ref, w_ref, kig_ref, vone_ref, q_ref, k_ref, v_ref, qi_ref, ki_ref, wi_ref):
    h = _rms(x_ref[...], g_ref[...]).astype(BF16)
    for j in range(NSLAB):
        sl = slice(LANES * j, LANES * (j + 1))
        q_ref[j] = (_dot(h, w_ref[:, sl]) * LOG2E).astype(BF16)
        k_ref[j] = _dot(h, w_ref[:, 1024 + LANES * j:1024 + LANES * (j + 1)]).astype(BF16)
        v_ref[j] = (_dot(h, w_ref[:, 2048 + LANES * j:2048 + LANES * (j + 1)])
                    + vone_ref[:, sl]).astype(BF16)
    for j in range(IDX_HEADS // 2):
        qi_ref[j] = _dot(h, w_ref[:, 3072 + LANES * j:3072 + LANES * (j + 1)]).astype(BF16)
    kia = _dot(h, w_ref[:, 3584:3712])
    kib = _dot(h, w_ref[:, 3712:3840])
    inv = lax.rsqrt(jnp.sum(kia * kia, axis=-1, keepdims=True) * (1.0 / IDX_HD) + EPS)
    ki_ref[0] = ((kia * inv) * kig_ref[:, 0:LANES]).astype(BF16)
    ki_ref[1] = ((kib * inv) * kig_ref[:, LANES:2 * LANES]).astype(BF16)
    wi_ref[...] = (_dot(h, w_ref[:, 3840:3968]) * (IDX_HEADS ** -0.5)) * (IDX_HD ** -0.5)


def _odd_in(x2, g, w, kig, vone, b, s):
    m = x2.shape[0]
    row = lambda n: pl.BlockSpec((TM, n), lambda i: (i, 0))
    full = lambda a: pl.BlockSpec(a.shape, lambda i: (0, 0))
    sds = jax.ShapeDtypeStruct
    slab = lambda n: sds((b, n, s, LANES), BF16)
    return pl.pallas_call(
        _odd_in_kernel,
        grid=(m // TM,),
        in_specs=[row(D_MODEL), full(g), full(w), full(kig), full(vone)],
        out_specs=[_slab_spec(NSLAB, TM, s), _slab_spec(NSLAB, TM, s), _slab_spec(NSLAB, TM, s),
                   _slab_spec(4, TM, s), _slab_spec(2, TM, s), row(LANES)],
        out_shape=[slab(NSLAB), slab(NSLAB), slab(NSLAB), slab(4), slab(2), sds((m, LANES), F32)],
        compiler_params=_params(("parallel",)),
        name="odd_in",
    )(x2, g, w, kig, vone)


def _order_key(sc):
    bits = pltpu.bitcast(sc, I32)
    key = jnp.where(bits < 0, bits ^ jnp.int32(0x7FFFFFFF), bits)
    return jnp.where(sc == 0.0, 0, key)


def _dsa_kernel(q_ref, k_ref, v_ref, qi_ref, ki_ref, wi_ref, tri_ref, o_ref,
                wb_scr, key_scr, bias_scr, o_scr, *, topk):
    c = pl.program_id(1)
    ntile = (c * TQ) // TK + 1

    wi = wi_ref[...]
    for hd in range(IDX_HEADS):
        wb_scr[hd] = jnp.broadcast_to(wi[:, hd:hd + 1], (TQ, LANES))

    def score_tile(kb, carry):
        rows = _key_rows(kb)
        lo = jnp.zeros((TQ, LANES), F32)
        hi = jnp.zeros((TQ, LANES), F32)
        for p in range(IDX_HEADS // 2):
            slab = qi_ref[p]
            for e in range(2):
                z = jnp.maximum(_dot_nt(slab, ki_ref[e, rows, :]), 0.0)
                w = wb_scr[2 * p + e]
                lo = lo + w * z[:, :LANES]
                hi = hi + w * z[:, LANES:]
        score = jnp.concatenate([lo, hi], axis=1)
        score = jnp.where(_chunk_admissible(c, kb), score, -jnp.inf)
        key_scr[kb] = _order_key(score)
        return carry

    lax.fori_loop(0, ntile, score_tile, 0)

    def count(pred):
        def tile(kb, acc):
            hit = jnp.where(pred(key_scr[kb]), 1.0, 0.0)
            return acc + hit[:, :LANES] + hit[:, LANES:]
        acc = lax.fori_loop(0, ntile, tile, jnp.zeros((TQ, LANES), F32))
        return jnp.sum(acc, axis=1, keepdims=True)

    def step(i, t):
        cand = t + jnp.left_shift(jnp.int32(1), 31 - i)
        return jnp.where(count(lambda k: k >= cand) >= float(topk), cand, t)

    t = lax.fori_loop(0, 32, step, jnp.full((TQ, 1), -2 ** 31, I32))

    need = float(topk) - count(lambda k: k > t)

    def bias_tile(kb, ties_before):
        key = key_scr[kb]
        eq = jnp.where(key == t, 1.0, 0.0)
        prefix = _dot(eq.astype(BF16), tri_ref[...]) + ties_before
        sel = jnp.where(key > t, 1.0, jnp.where(prefix < need, eq, 0.0))
        sel = jnp.where(_chunk_admissible(c, kb), sel, 0.0)
        bias_scr[kb] = jnp.where(sel > 0.0, 0.0, NEG)
        return ties_before + jnp.sum(eq, axis=1, keepdims=True)

    lax.fori_loop(0, ntile, bias_tile, jnp.zeros((TQ, 1), F32))

    def pair(gp, carry):
        g = gp // 2
        q = q_ref[gp]
        accs = []
        for e in range(2):
            sidx = e * DSA_KV_HEADS + g

            def tile(kb, fl, sidx=sidx):
                rows = _key_rows(kb)
                return _flash_tile(q, k_ref[sidx, rows, :], v_ref[sidx, rows, :], bias_scr[kb], fl)

            accs.append(lax.fori_loop(0, ntile, tile, _flash_init())[1])
        o_scr[gp] = _pair_slab(accs[0], accs[1])
        return carry

    lax.fori_loop(0, NSLAB, pair, 0)
    for gp in range(NSLAB):
        o_ref[:, LANES * gp:LANES * (gp + 1)] = o_scr[gp]


def _dsa(q, k, v, qi, ki, wi, tri):
    b, _, s, _ = q.shape
    topk = min(TOPK_MAX, s // 4)
    assert topk <= TK and s % TK == 0 and TK % TQ == 0
    qblk = lambda n: pl.BlockSpec((None, n, TQ, LANES), lambda i, j: (i, 0, j, 0))
    kblk = lambda n: pl.BlockSpec((None, n, s, LANES), lambda i, j: (i, 0, 0, 0))
    nkb = s // TK
    return pl.pallas_call(
        functools.partial(_dsa_kernel, topk=topk),
        grid=(b, s // TQ),
        in_specs=[qblk(NSLAB), kblk(NSLAB), kblk(NSLAB), qblk(4), kblk(2),
                  pl.BlockSpec((None, TQ, LANES), lambda i, j: (i, j, 0)),
                  pl.BlockSpec((TK, TK), lambda i, j: (0, 0))],
        out_specs=pl.BlockSpec((None, TQ, 1024), lambda i, j: (i, j, 0)),
        out_shape=jax.ShapeDtypeStruct((b, s, 1024), BF16),
        scratch_shapes=[pltpu.VMEM((IDX_HEADS, TQ, LANES), F32), pltpu.VMEM((nkb, TQ, TK), I32),
                        pltpu.VMEM((nkb, TQ, TK), F32), pltpu.VMEM((NSLAB, TQ, LANES), BF16)],
        compiler_params=_params(("parallel", "arbitrary")),
        name="dsa_attn",
    )(q, k, v, qi, ki, wi, tri)


def _out_kernel(*refs):
    x_ref, o_ref = refs[0], refs[-1]
    acc = x_ref[...]
    for a_ref, w_ref in zip(refs[1:-1:2], refs[2:-1:2]):
        acc = acc + _dot(a_ref[...], w_ref[...])
    o_ref[...] = acc


def _out_proj(x2, pairs):
    m = x2.shape[0]
    row = lambda n: pl.BlockSpec((TM, n), lambda i: (i, 0))
    full = lambda a: pl.BlockSpec(a.shape, lambda i: (0, 0))
    specs, args = [row(D_MODEL)], [x2]
    for a, w in pairs:
        specs += [row(a.shape[1]), full(w)]
        args += [a, w]
    return pl.pallas_call(
        _out_kernel,
        grid=(m // TM,),
        in_specs=specs,
        out_specs=row(D_MODEL),
        out_shape=jax.ShapeDtypeStruct((m, D_MODEL), F32),
        compiler_params=_params(("parallel",)),
        name="out_proj",
    )(*args)


def _mlp_kernel(x_ref, g_ref, w1_ref, w2_ref, fg_ref, o_ref, h_scr, acc, *, final):
    j = pl.program_id(1)

    @pl.when(j == 0)
    def _():
        h_scr[...] = _rms(x_ref[...], g_ref[...]).astype(BF16)
        acc[...] = jnp.zeros_like(acc)

    a = jnp.maximum(_dot(h_scr[...], w1_ref[...]), 0.0)
    acc[...] += _dot((a * a).astype(BF16), w2_ref[...])

    @pl.when(j == pl.num_programs(1) - 1)
    def _():
        y = x_ref[...] + acc[...]
        o_ref[...] = _rms(y, fg_ref[...]) if final else y


def _mlp(x2, g, w1, w2, fg, final):
    m = x2.shape[0]
    return pl.pallas_call(
        functools.partial(_mlp_kernel, final=final),
        grid=(m // MLP_TM, D_FF // MLP_TF),
        in_specs=[pl.BlockSpec((MLP_TM, D_MODEL), lambda i, j: (i, 0)),
                  pl.BlockSpec((1, D_MODEL), lambda i, j: (0, 0)),
                  pl.BlockSpec((D_MODEL, MLP_TF), lambda i, j: (0, j)),
                  pl.BlockSpec((MLP_TF, D_MODEL), lambda i, j: (j, 0)),
                  pl.BlockSpec((1, D_MODEL), lambda i, j: (0, 0))],
        out_specs=pl.BlockSpec((MLP_TM, D_MODEL), lambda i, j: (i, 0)),
        out_shape=jax.ShapeDtypeStruct((m, D_MODEL), F32),
        scratch_shapes=[pltpu.VMEM((MLP_TM, D_MODEL), BF16), pltpu.VMEM((MLP_TM, D_MODEL), F32)],
        compiler_params=_params(("parallel", "arbitrary")),
        name="mlp",
    )(x2, g, w1, w2, fg)


def _rot_cols(w):
    half = w.shape[-1] // 2
    return jnp.concatenate([-w[..., half:], w[..., :half]], axis=-1)


def _one_hot_row(n, idx):
    return jnp.zeros((1, n), F32).at[0, jnp.asarray(idx)].set(1.0)


def _even_weights(w_in, w_uq, w_ukv):
    d = w_in.shape[0]
    wkr = w_in[:, 1408:1440]
    z64 = jnp.zeros((d, 64), F32)
    w = jnp.concatenate([w_in[:, :1408], z64, wkr, wkr, z64, _rot_cols(wkr), _rot_cols(wkr)], axis=1)
    uq = w_uq.reshape(MLA_Q_RANK, MLA_HEADS, MLA_NOPE + MLA_ROPE)
    rope_w = uq[..., MLA_NOPE:]
    wq = jnp.concatenate([uq[..., :MLA_NOPE], rope_w, _rot_cols(rope_w)], axis=-1)
    wq = wq.reshape(MLA_Q_RANK, MLA_HEADS * LANES)
    ukv = w_ukv.reshape(MLA_KV_RANK, MLA_HEADS, MLA_NOPE + MLA_V)
    zk = jnp.zeros((MLA_KV_RANK, MLA_HEADS, 64), F32)
    wk = jnp.concatenate([ukv[..., :MLA_NOPE], zk], axis=-1).reshape(MLA_KV_RANK, MLA_HEADS * LANES)
    vv = ukv[..., MLA_NOPE:].reshape(MLA_KV_RANK, MLA_HEADS // 2, 2, MLA_V)
    zv = jnp.zeros((MLA_KV_RANK, MLA_HEADS // 2, MLA_V), F32)
    v_even = jnp.concatenate([vv[:, :, 0], zv], axis=-1)
    v_odd = jnp.concatenate([zv, vv[:, :, 1]], axis=-1)
    wv = jnp.stack([v_even, v_odd], axis=2).reshape(MLA_KV_RANK, MLA_HEADS * LANES)
    wkv = jnp.concatenate([wk, wv], axis=1)
    return w.astype(BF16), wq.astype(BF16), wkv.astype(BF16)


def _odd_weights(w_in):
    d = w_in.shape[0]
    wq = w_in[:, :1024] * (DSA_HD ** -0.5)
    wk = w_in[:, 1024:1280].reshape(d, DSA_KV_HEADS, DSA_HD)
    wv = w_in[:, 1280:1536].reshape(d, DSA_KV_HEADS, DSA_HD)
    z = jnp.zeros((d, DSA_KV_HEADS, DSA_HD), F32)
    two = lambda a: jnp.concatenate([jnp.concatenate([a, z], -1).reshape(d, -1),
                                     jnp.concatenate([z, a], -1).reshape(d, -1)], axis=1)
    wqi = w_in[:, 1536:2048]
    wki = w_in[:, 2048:2112]
    z64 = jnp.zeros((d, 64), F32)
    wwi = jnp.concatenate([w_in[:, 2112:2120], jnp.zeros((d, LANES - IDX_HEADS), F32)], axis=1)
    w = jnp.concatenate([wq, two(wk), two(wv), wqi, wki, z64, z64, wki, wwi], axis=1)
    return w.astype(BF16)


def _block_diag(w):
    n, k, _ = w.shape
    eye = jnp.eye(n, dtype=w.dtype)
    return (w[:, :, None, :] * eye[:, None, :, None]).reshape(n * k, n * k)


def kernel(x, positions, e_norm, e_w_in, e_conv_w, e_conv_b, e_ga_w, e_ga_b, e_gx_w, e_gx_b, e_lambda,
           e_q_norm, e_w_uq, e_kv_norm, e_w_ukv, e_w_out, o_norm, o_w_in, o_idx_k_norm, o_w_out,
           m_norm, m_w1, m_w2, final_norm):
    b, s, d = x.shape
    m = b * s
    row = lambda v: v.reshape(1, -1).astype(F32)

    freqs = ROPE_THETA ** (-jnp.arange(0, MLA_ROPE, 2, dtype=F32) / MLA_ROPE)
    ang = positions.astype(F32)[..., None] * freqs
    cos2 = jnp.tile(jnp.cos(ang), (1, 1, 2)).reshape(m, MLA_ROPE)
    sin2 = jnp.tile(jnp.sin(ang), (1, 1, 2)).reshape(m, MLA_ROPE)
    z64 = jnp.zeros((m, 64), F32)
    scale = LOG2E * (MLA_NOPE + MLA_ROPE) ** -0.5
    tq = scale * jnp.concatenate([jnp.ones((m, 64), F32), cos2, sin2], axis=1)
    tc = jnp.concatenate([z64, cos2, cos2], axis=1)
    ts = jnp.concatenate([z64, sin2, sin2], axis=1)
    vone_even = _one_hot_row(MLA_HEADS * LANES, [LANES * h + (64 if h % 2 == 0 else 0) for h in range(MLA_HEADS)])
    vone_odd = _one_hot_row(2 * DSA_KV_HEADS * LANES,
                            [LANES * g + 64 for g in range(DSA_KV_HEADS)]
                            + [512 + LANES * g for g in range(DSA_KV_HEADS)])
    tri = (jnp.arange(TK)[:, None] < jnp.arange(TK)[None, :]).astype(BF16)

    x2 = x.reshape(m, d)
    for l in range(DEPTH):
        j = l // 2
        if l % 2 == 0:
            w, wq, wkv = _even_weights(e_w_in[j], e_w_uq[j], e_w_ukv[j])
            xr, gr, q, k, v = _even_in(x2, row(e_norm[j]), w, row(e_q_norm[j]), wq, row(e_kv_norm[j]), wkv,
                                       tq, tc, ts, vone_even, b, s)
            cl = -LRU_C * jax.nn.softplus(-e_lambda[j].astype(F32))
            y_lru = _lru(xr.reshape(b, s, -1), gr.reshape(b, s, -1), e_conv_w[j].astype(F32),
                         row(e_conv_b[j]), _block_diag(e_ga_w[j]).astype(BF16), row(e_ga_b[j]),
                         _block_diag(e_gx_w[j]).astype(BF16), row(e_gx_b[j]), row(cl))
            y_mla = _mla(q, k, v)
            wo = e_w_out[j].astype(BF16)
            x2 = _out_proj(x2, [(y_lru.reshape(m, -1), wo[:LRU_WIDTH]), (y_mla.reshape(m, -1), wo[LRU_WIDTH:])])
        else:
            w = _odd_weights(o_w_in[j])
            kig = jnp.concatenate([row(o_idx_k_norm[j]), jnp.zeros((1, 128), F32), row(o_idx_k_norm[j])], axis=1)
            q, k, v, qi, ki, wi = _odd_in(x2, row(o_norm[j]), w, kig, vone_odd, b, s)
            y = _dsa(q, k, v, qi, ki, wi.reshape(b, s, -1), tri)
            x2 = _out_proj(x2, [(y.reshape(m, -1), o_w_out[j].astype(BF16))])
        x2 = _mlp(x2, row(m_norm[l]), m_w1[l].astype(BF16), m_w2[l].astype(BF16), row(final_norm),
                  final=(l == DEPTH - 1))
    return x2.reshape(b, s, d)
```

```python
import functools
import math

import jax
import jax.numpy as jnp
from jax import lax
from jax.experimental import pallas as pl
from jax.experimental.pallas import tpu as pltpu

F32 = jnp.float32
BF16 = jnp.bfloat16
I32 = jnp.int32

D_MODEL = 1024
DEPTH = 4
CHUNK = 64
EPS = 1e-6

LRU_WIDTH = D_MODEL // 2
LRU_BLOCKS = 8
CONV_W = 4
LRU_C = 8.0

MLA_HEADS = 8
MLA_NOPE = 64
MLA_ROPE = 32
MLA_V = 64
MLA_Q_RANK = D_MODEL // 4
MLA_KV_RANK = D_MODEL // 8
ROPE_THETA = 10000.0

DSA_HEADS = 16
DSA_KV_HEADS = 4
DSA_HD = 64
IDX_HEADS = 8
IDX_HD = 64
TOPK_MAX = 256

D_FF = 4 * D_MODEL

LANES = 128
SUBLANES = 8
VMEM_LIMIT = 56 * 1024 * 1024
LOG2E = 1.4426950408889634
NEG = -1e30

TM = 512
TQ = 256
TK = 256
LRU_T = 256
MLP_TM = 1024
MLP_TF = 512
NSLAB = 8


def _params(sem):
    return pltpu.CompilerParams(dimension_semantics=sem, vmem_limit_bytes=VMEM_LIMIT)


def _rms(x, g):
    ms = jnp.mean(x * x, axis=-1, keepdims=True)
    return (x * lax.rsqrt(ms + EPS)) * g


def _dot(a, b):
    return jnp.dot(a, b, preferred_element_type=F32)


def _dot_nt(a, b):
    return lax.dot_general(a, b, (((1,), (1,)), ((), ())), preferred_element_type=F32)


def _slab_spec(nslab, rows, seq):
    nt = seq // rows
    return pl.BlockSpec((None, nslab, rows, LANES), lambda i: (i // nt, 0, i % nt, 0))


def _even_in_kernel(x_ref, g_ref, w_ref, qn_ref, wq_ref, kvn_ref, wkv_ref, tq_ref, tc_ref, ts_ref,
                    vone_ref, xr_ref, gr_ref, q_ref, k_ref, v_ref):
    h = _rms(x_ref[...], g_ref[...]).astype(BF16)
    xr_ref[...] = _dot(h, w_ref[:, 0:512])
    gr_ref[...] = _dot(h, w_ref[:, 512:1024])
    cq = _dot(h, w_ref[:, 1024:1280])
    ckv = _dot(h, w_ref[:, 1280:1408])
    kr = _dot(h, w_ref[:, 1408:1536])
    krs = _dot(h, w_ref[:, 1536:1664])
    krope = kr * tc_ref[...] + krs * ts_ref[...]
    cqn = _rms(cq, qn_ref[...]).astype(BF16)
    ckvn = _rms(ckv, kvn_ref[...]).astype(BF16)
    tq = tq_ref[...]
    for hd in range(MLA_HEADS):
        sl = slice(LANES * hd, LANES * (hd + 1))
        q_ref[hd] = (_dot(cqn, wq_ref[:, sl]) * tq).astype(BF16)
        k_ref[hd] = (_dot(ckvn, wkv_ref[:, sl]) + krope).astype(BF16)
        vs = slice(1024 + LANES * hd, 1024 + LANES * (hd + 1))
        v_ref[hd] = (_dot(ckvn, wkv_ref[:, vs]) + vone_ref[:, sl]).astype(BF16)


def _even_in(x2, g, w, qn, wq, kvn, wkv, tq, tc, ts, vone, b, s):
    m = x2.shape[0]
    row = lambda n: pl.BlockSpec((TM, n), lambda i: (i, 0))
    full = lambda a: pl.BlockSpec(a.shape, lambda i: (0, 0))
    slab = _slab_spec(NSLAB, TM, s)
    sds = jax.ShapeDtypeStruct
    return pl.pallas_call(
        _even_in_kernel,
        grid=(m // TM,),
        in_specs=[row(D_MODEL), full(g), full(w), full(qn), full(wq), full(kvn), full(wkv),
                  row(LANES), row(LANES), row(LANES), full(vone)],
        out_specs=[row(512), row(512), slab, slab, slab],
        out_shape=[sds((m, 512), F32), sds((m, 512), F32), sds((b, NSLAB, s, LANES), BF16),
                   sds((b, NSLAB, s, LANES), BF16), sds((b, NSLAB, s, LANES), BF16)],
        compiler_params=_params(("parallel",)),
        name="even_in",
    )(x2, g, w, qn, wq, kvn, wkv, tq, tc, ts, vone)


def _gelu_tanh(x):
    c = math.sqrt(2.0 / math.pi)
    return 0.5 * x * (1.0 + jnp.tanh(c * (x + 0.044715 * (x * x * x))))


def _lru_kernel(xr_ref, gr_ref, cw_ref, cb_ref, wa_ref, ba_ref, wx_ref, bx_ref, cl_ref, y_ref,
                xbuf, hc):
    t = LRU_T

    @pl.when(pl.program_id(1) == 0)
    def _():
        xbuf[0:8, :] = jnp.zeros((8, LRU_WIDTH), F32)
        hc[...] = jnp.zeros((1, LRU_WIDTH), F32)

    x = xr_ref[...]
    xbuf[8:8 + t, :] = x
    xc = cb_ref[...] + cw_ref[3:4, :] * x
    for j in range(CONV_W - 1):
        xc = xc + cw_ref[j:j + 1, :] * xbuf[5 + j:5 + j + t, :]
    xbuf[0:8, :] = x[t - 8:t, :]

    xcb = xc.astype(BF16)
    r = jax.nn.sigmoid(_dot(xcb, wa_ref[...]) + ba_ref[...])
    i = jax.nn.sigmoid(_dot(xcb, wx_ref[...]) + bx_ref[...])
    log_a = cl_ref[...] * r
    a = jnp.exp(log_a)
    th = jnp.tanh(log_a)
    b = jnp.sqrt((-2.0 * th) / (1.0 - th)) * (i * xc)

    rows = lax.broadcasted_iota(I32, (t, LRU_WIDTH), 0)
    d = 1
    while d < t:
        a_sh = pltpu.roll(a, d, 0)
        b_sh = pltpu.roll(b, d, 0)
        keep = rows >= d
        b = jnp.where(keep, a * b_sh + b, b)
        a = jnp.where(keep, a * a_sh, a)
        d *= 2
    h = a * hc[...] + b
    hc[...] = h[t - 1:t, :]
    y_ref[...] = (h * _gelu_tanh(gr_ref[...])).astype(BF16)


def _lru(xr, gr, cw, cb, wa, ba, wx, bx, cl):
    b, s, w = xr.shape
    blk = pl.BlockSpec((None, LRU_T, w), lambda i, j: (i, j, 0))
    full = lambda a: pl.BlockSpec(a.shape, lambda i, j: (0, 0))
    return pl.pallas_call(
        _lru_kernel,
        grid=(b, s // LRU_T),
        in_specs=[blk, blk, full(cw), full(cb), full(wa), full(ba), full(wx), full(bx), full(cl)],
        out_specs=blk,
        out_shape=jax.ShapeDtypeStruct((b, s, w), BF16),
        scratch_shapes=[pltpu.VMEM((LRU_T + 8, w), F32), pltpu.VMEM((1, w), F32)],
        compiler_params=_params(("parallel", "arbitrary")),
        name="rg_lru",
    )(xr, gr, cw, cb, wa, ba, wx, bx, cl)


def _key_rows(kb):
    return pl.ds(pl.multiple_of(kb * TK, TK), TK)


def _chunk_of(pos):
    return pos >> (CHUNK.bit_length() - 1)


def _admissible(c, kb):
    rc = _chunk_of(lax.broadcasted_iota(I32, (TQ, 1), 0) + c * TQ)
    cc = _chunk_of(lax.broadcasted_iota(I32, (1, TK), 1) + kb * TK)
    return cc <= rc


def _admissible_t(c, kb):
    kc = _chunk_of(lax.broadcasted_iota(I32, (TK, 1), 0) + kb * TK)
    qc = _chunk_of(lax.broadcasted_iota(I32, (1, TQ), 1) + c * TQ)
    return kc <= qc


def _flash_reset(m_scr, acc_scr):
    m_scr[...] = jnp.full(m_scr.shape, -jnp.inf, F32)
    acc_scr[...] = jnp.zeros(acc_scr.shape, F32)


def _flash_step(q, k, v, bias, m_ref, acc_ref):
    s = _dot_nt(q, k)
    if bias is not None:
        s = s + bias
    m_old = m_ref[...]
    m_new = jnp.maximum(m_old, jnp.max(s, axis=1, keepdims=True))
    p = jnp.exp2(s - jnp.tile(m_new, (1, s.shape[1] // LANES))).astype(BF16)
    acc_ref[...] = jnp.exp2(m_old - m_new) * acc_ref[...] + _dot(p, v)
    m_ref[...] = m_new


def _pair_slab(acc_e, acc_o):
    lane = lax.broadcasted_iota(I32, acc_e.shape, 1)
    oe = acc_e * (1.0 / acc_e[:, 64:65])
    oo = acc_o * (1.0 / acc_o[:, 0:1])
    return jnp.where(lane < 64, oe, oo).astype(BF16)


def _mla_kernel(q_ref, k_ref, v_ref, o_ref, m_scr, acc_scr):
    c = pl.program_id(1)
    kd = (c * TQ) // TK
    _flash_reset(m_scr, acc_scr)

    def tile(kb, bias):
        rows = _key_rows(kb)
        for hd in range(MLA_HEADS):
            _flash_step(q_ref[hd], k_ref[hd, rows, :], v_ref[hd, rows, :], bias,
                        m_scr.at[hd], acc_scr.at[hd])

    def past_tile(kb, carry):
        tile(kb, None)
        return carry

    lax.fori_loop(0, kd, past_tile, 0)
    tile(kd, jnp.where(_admissible(c, kd), 0.0, NEG).astype(F32))
    for p in range(MLA_HEADS // 2):
        o_ref[:, LANES * p:LANES * (p + 1)] = _pair_slab(acc_scr[2 * p], acc_scr[2 * p + 1])


def _mla(q, k, v):
    b, _, s, _ = q.shape
    assert TQ % TK == 0 or TK % TQ == 0
    return pl.pallas_call(
        _mla_kernel,
        grid=(b, s // TQ),
        in_specs=[pl.BlockSpec((None, NSLAB, TQ, LANES), lambda i, j: (i, 0, j, 0)),
                  pl.BlockSpec((None, NSLAB, s, LANES), lambda i, j: (i, 0, 0, 0)),
                  pl.BlockSpec((None, NSLAB, s, LANES), lambda i, j: (i, 0, 0, 0))],
        out_specs=pl.BlockSpec((None, TQ, 512), lambda i, j: (i, j, 0)),
        out_shape=jax.ShapeDtypeStruct((b, s, 512), BF16),
        scratch_shapes=[pltpu.VMEM((MLA_HEADS, TQ, LANES), F32), pltpu.VMEM((MLA_HEADS, TQ, LANES), F32)],
        compiler_params=_params(("parallel", "arbitrary")),
        name="mla_attn",
    )(q, k, v)


def _odd_in_kernel(x_ref, g_ref, w_ref, wwt_ref, kig_ref, vone_ref,
                   q_ref, k_ref, v_ref, qi_ref, ki_ref, wit_ref):
    h = _rms(x_ref[...], g_ref[...]).astype(BF16)
    for j in range(NSLAB):
        sl = slice(LANES * j, LANES * (j + 1))
        q_ref[j] = (_dot(h, w_ref[:, sl]) * LOG2E).astype(BF16)
        k_ref[j] = _dot(h, w_ref[:, 1024 + LANES * j:1024 + LANES * (j + 1)]).astype(BF16)
        v_ref[j] = (_dot(h, w_ref[:, 2048 + LANES * j:2048 + LANES * (j + 1)])
                    + vone_ref[:, sl]).astype(BF16)
    for j in range(IDX_HEADS // 2):
        qi_ref[j] = _dot(h, w_ref[:, 3072 + LANES * j:3072 + LANES * (j + 1)]).astype(BF16)
    kia = _dot(h, w_ref[:, 3584:3712])
    kib = _dot(h, w_ref[:, 3712:3840])
    inv = lax.rsqrt(jnp.sum(kia * kia, axis=-1, keepdims=True) * (1.0 / IDX_HD) + EPS)
    ki_ref[0] = ((kia * inv) * kig_ref[:, 0:LANES]).astype(BF16)
    ki_ref[1] = ((kib * inv) * kig_ref[:, LANES:2 * LANES]).astype(BF16)
    wit = _dot_nt(wwt_ref[...], h)[0:IDX_HEADS, :]
    wit_ref[...] = (wit * (IDX_HEADS ** -0.5)) * (IDX_HD ** -0.5)


def _odd_in(x2, g, w, wwt, kig, vone, b, s):
    m = x2.shape[0]
    row = lambda n: pl.BlockSpec((TM, n), lambda i: (i, 0))
    full = lambda a: pl.BlockSpec(a.shape, lambda i: (0, 0))
    sds = jax.ShapeDtypeStruct
    slab = lambda n: sds((b, n, s, LANES), BF16)
    return pl.pallas_call(
        _odd_in_kernel,
        grid=(m // TM,),
        in_specs=[row(D_MODEL), full(g), full(w), full(wwt), full(kig), full(vone)],
        out_specs=[_slab_spec(NSLAB, TM, s), _slab_spec(NSLAB, TM, s), _slab_spec(NSLAB, TM, s),
                   _slab_spec(4, TM, s), _slab_spec(2, TM, s),
                   pl.BlockSpec((IDX_HEADS, TM), lambda i: (0, i))],
        out_shape=[slab(NSLAB), slab(NSLAB), slab(NSLAB), slab(4), slab(2), sds((IDX_HEADS, m), F32)],
        compiler_params=_params(("parallel",)),
        name="odd_in",
    )(x2, g, w, wwt, kig, vone)


def _order_key(sc):
    bits = pltpu.bitcast(sc, I32)
    key = jnp.where(bits < 0, bits ^ jnp.int32(0x7FFFFFFF), bits)
    return jnp.where(sc == 0.0, 0, key)


def _dsa_kernel(q_ref, k_ref, v_ref, qi_ref, ki_ref, wit_ref, low_ref, o_ref,
                key_scr, bias_scr, m_scr, acc_scr, *, topk):
    c = pl.program_id(1)
    ntile = (c * TQ) // TK + 1
    part = 4 * SUBLANES

    def score_tile(kb, carry):
        rows = _key_rows(kb)
        score = jnp.zeros((TK, TQ), F32)
        for p in range(IDX_HEADS // 2):
            slab = qi_ref[p]
            for e in range(2):
                hd = 2 * p + e
                z = jnp.maximum(_dot_nt(ki_ref[e, rows, :], slab), 0.0)
                score = score + wit_ref[hd:hd + 1, :] * z
        score = jnp.where(_admissible_t(c, kb), score, -jnp.inf)
        key_scr[kb] = _order_key(score)
        return carry

    lax.fori_loop(0, ntile, score_tile, 0)

    def count(pred):
        def tile(kb, acc):
            hit = jnp.where(pred(key_scr[kb]), 1.0, 0.0)
            return acc + hit.reshape(TK // part, part, TQ).sum(axis=0)
        acc = lax.fori_loop(0, ntile, tile, jnp.zeros((part, TQ), F32))
        return jnp.sum(acc, axis=0, keepdims=True)

    def step(i, t):
        cand = t + jnp.left_shift(jnp.int32(1), 31 - i)
        return jnp.where(count(lambda k: k >= cand) >= float(topk), cand, t)

    t = lax.fori_loop(0, 32, step, jnp.full((1, TQ), -2 ** 31, I32))

    need = float(topk) - count(lambda k: k > t)

    def bias_tile(kb, ties_before):
        key = key_scr[kb]
        eq = jnp.where(key == t, 1.0, 0.0)
        prefix = _dot(low_ref[...], eq.astype(BF16)) + ties_before
        sel = jnp.where(key > t, 1.0, jnp.where(prefix < need, eq, 0.0))
        sel = jnp.where(_admissible_t(c, kb), sel, 0.0)
        bias_scr[kb] = jnp.where(sel > 0.0, 0.0, NEG).T
        return ties_before + jnp.sum(eq, axis=0, keepdims=True)

    lax.fori_loop(0, ntile, bias_tile, jnp.zeros((1, TQ), F32))

    _flash_reset(m_scr, acc_scr)

    def attn_tile(kb, carry):
        rows = _key_rows(kb)
        bias = bias_scr[kb]
        bias2 = jnp.concatenate([bias, bias], axis=0)
        for g in range(DSA_KV_HEADS):
            q2 = q_ref[2 * g:2 * g + 2].reshape(2 * TQ, LANES)
            for e in range(2):
                ch = 2 * g + e
                sidx = e * DSA_KV_HEADS + g
                _flash_step(q2, k_ref[sidx, rows, :], v_ref[sidx, rows, :], bias2,
                            m_scr.at[ch], acc_scr.at[ch])
        return carry

    lax.fori_loop(0, ntile, attn_tile, 0)
    for gp in range(NSLAB):
        g, half = gp // 2, gp % 2
        rs = slice(half * TQ, (half + 1) * TQ)
        o_ref[:, LANES * gp:LANES * (gp + 1)] = _pair_slab(acc_scr[2 * g, rs, :], acc_scr[2 * g + 1, rs, :])


def _dsa(q, k, v, qi, ki, wit, low):
    b, _, s, _ = q.shape
    topk = min(TOPK_MAX, s // 4)
    assert topk <= TK and s % TK == 0 and TK % TQ == 0
    qblk = lambda n: pl.BlockSpec((None, n, TQ, LANES), lambda i, j: (i, 0, j, 0))
    kblk = lambda n: pl.BlockSpec((None, n, s, LANES), lambda i, j: (i, 0, 0, 0))
    nkb, nq = s // TK, s // TQ
    return pl.pallas_call(
        functools.partial(_dsa_kernel, topk=topk),
        grid=(b, nq),
        in_specs=[qblk(NSLAB), kblk(NSLAB), kblk(NSLAB), qblk(4), kblk(2),
                  pl.BlockSpec((IDX_HEADS, TQ), lambda i, j: (0, i * nq + j)),
                  pl.BlockSpec((TK, TK), lambda i, j: (0, 0))],
        out_specs=pl.BlockSpec((None, TQ, 1024), lambda i, j: (i, j, 0)),
        out_shape=jax.ShapeDtypeStruct((b, s, 1024), BF16),
        scratch_shapes=[pltpu.VMEM((nkb, TK, TQ), I32), pltpu.VMEM((nkb, TQ, TK), F32),
                        pltpu.VMEM((NSLAB, 2 * TQ, LANES), F32), pltpu.VMEM((NSLAB, 2 * TQ, LANES), F32)],
        compiler_params=_params(("parallel", "arbitrary")),
        name="dsa_attn",
    )(q, k, v, qi, ki, wit, low)


def _out_kernel(*refs):
    x_ref, o_ref = refs[0], refs[-1]
    acc = x_ref[...]
    for a_ref, w_ref in zip(refs[1:-1:2], refs[2:-1:2]):
        acc = acc + _dot(a_ref[...], w_ref[...])
    o_ref[...] = acc


def _out_proj(x2, pairs):
    m = x2.shape[0]
    row = lambda n: pl.BlockSpec((TM, n), lambda i: (i, 0))
    full = lambda a: pl.BlockSpec(a.shape, lambda i: (0, 0))
    specs, args = [row(D_MODEL)], [x2]
    for a, w in pairs:
        specs += [row(a.shape[1]), full(w)]
        args += [a, w]
    return pl.pallas_call(
        _out_kernel,
        grid=(m // TM,),
        in_specs=specs,
        out_specs=row(D_MODEL),
        out_shape=jax.ShapeDtypeStruct((m, D_MODEL), F32),
        compiler_params=_params(("parallel",)),
        name="out_proj",
    )(*args)


def _mlp_kernel(x_ref, g_ref, w1_ref, w2_ref, fg_ref, o_ref, h_scr, acc, *, final):
    j = pl.program_id(1)

    @pl.when(j == 0)
    def _():
        h_scr[...] = _rms(x_ref[...], g_ref[...]).astype(BF16)
        acc[...] = jnp.zeros_like(acc)

    a = jnp.maximum(_dot(h_scr[...], w1_ref[...]), 0.0)
    acc[...] += _dot((a * a).astype(BF16), w2_ref[...])

    @pl.when(j == pl.num_programs(1) - 1)
    def _():
        y = x_ref[...] + acc[...]
        o_ref[...] = _rms(y, fg_ref[...]) if final else y


def _mlp(x2, g, w1, w2, fg, final):
    m = x2.shape[0]
    return pl.pallas_call(
        functools.partial(_mlp_kernel, final=final),
        grid=(m // MLP_TM, D_FF // MLP_TF),
        in_specs=[pl.BlockSpec((MLP_TM, D_MODEL), lambda i, j: (i, 0)),
                  pl.BlockSpec((1, D_MODEL), lambda i, j: (0, 0)),
                  pl.BlockSpec((D_MODEL, MLP_TF), lambda i, j: (0, j)),
                  pl.BlockSpec((MLP_TF, D_MODEL), lambda i, j: (j, 0)),
                  pl.BlockSpec((1, D_MODEL), lambda i, j: (0, 0))],
        out_specs=pl.BlockSpec((MLP_TM, D_MODEL), lambda i, j: (i, 0)),
        out_shape=jax.ShapeDtypeStruct((m, D_MODEL), F32),
        scratch_shapes=[pltpu.VMEM((MLP_TM, D_MODEL), BF16), pltpu.VMEM((MLP_TM, D_MODEL), F32)],
        compiler_params=_params(("parallel", "arbitrary")),
        name="mlp",
    )(x2, g, w1, w2, fg)


def _rot_cols(w):
    half = w.shape[-1] // 2
    return jnp.concatenate([-w[..., half:], w[..., :half]], axis=-1)


def _one_hot_row(n, idx):
    return jnp.zeros((1, n), F32).at[0, jnp.asarray(idx)].set(1.0)


def _even_weights(w_in, w_uq, w_ukv):
    d = w_in.shape[0]
    wkr = w_in[:, 1408:1440]
    z64 = jnp.zeros((d, 64), F32)
    w = jnp.concatenate([w_in[:, :1408], z64, wkr, wkr, z64, _rot_cols(wkr), _rot_cols(wkr)], axis=1)
    uq = w_uq.reshape(MLA_Q_RANK, MLA_HEADS, MLA_NOPE + MLA_ROPE)
    rope_w = uq[..., MLA_NOPE:]
    wq = jnp.concatenate([uq[..., :MLA_NOPE], rope_w, _rot_cols(rope_w)], axis=-1)
    wq = wq.reshape(MLA_Q_RANK, MLA_HEADS * LANES)
    ukv = w_ukv.reshape(MLA_KV_RANK, MLA_HEADS, MLA_NOPE + MLA_V)
    zk = jnp.zeros((MLA_KV_RANK, MLA_HEADS, 64), F32)
    wk = jnp.concatenate([ukv[..., :MLA_NOPE], zk], axis=-1).reshape(MLA_KV_RANK, MLA_HEADS * LANES)
    vv = ukv[..., MLA_NOPE:].reshape(MLA_KV_RANK, MLA_HEADS // 2, 2, MLA_V)
    zv = jnp.zeros((MLA_KV_RANK, MLA_HEADS // 2, MLA_V), F32)
    v_even = jnp.concatenate([vv[:, :, 0], zv], axis=-1)
    v_odd = jnp.concatenate([zv, vv[:, :, 1]], axis=-1)
    wv = jnp.stack([v_even, v_odd], axis=2).reshape(MLA_KV_RANK, MLA_HEADS * LANES)
    wkv = jnp.concatenate([wk, wv], axis=1)
    return w.astype(BF16), wq.astype(BF16), wkv.astype(BF16)


def _odd_weights(w_in):
    d = w_in.shape[0]
    wq = w_in[:, :1024] * (DSA_HD ** -0.5)
    wk = w_in[:, 1024:1280].reshape(d, DSA_KV_HEADS, DSA_HD)
    wv = w_in[:, 1280:1536].reshape(d, DSA_KV_HEADS, DSA_HD)
    z = jnp.zeros((d, DSA_KV_HEADS, DSA_HD), F32)
    two = lambda a: jnp.concatenate([jnp.concatenate([a, z], -1).reshape(d, -1),
                                     jnp.concatenate([z, a], -1).reshape(d, -1)], axis=1)
    wqi = w_in[:, 1536:2048]
    wki = w_in[:, 2048:2112]
    z64 = jnp.zeros((d, 64), F32)
    w = jnp.concatenate([wq, two(wk), two(wv), wqi, wki, z64, z64, wki], axis=1)
    wwt = jnp.concatenate([w_in[:, 2112:2120].T, jnp.zeros((2 * SUBLANES - IDX_HEADS, d), F32)], axis=0)
    return w.astype(BF16), wwt.astype(BF16)


def _block_diag(w):
    n, k, _ = w.shape
    eye = jnp.eye(n, dtype=w.dtype)
    return (w[:, :, None, :] * eye[:, None, :, None]).reshape(n * k, n * k)


def kernel(x, positions, e_norm, e_w_in, e_conv_w, e_conv_b, e_ga_w, e_ga_b, e_gx_w, e_gx_b, e_lambda,
           e_q_norm, e_w_uq, e_kv_norm, e_w_ukv, e_w_out, o_norm, o_w_in, o_idx_k_norm, o_w_out,
           m_norm, m_w1, m_w2, final_norm):
    b, s, d = x.shape
    m = b * s
    row = lambda v: v.reshape(1, -1).astype(F32)

    freqs = ROPE_THETA ** (-jnp.arange(0, MLA_ROPE, 2, dtype=F32) / MLA_ROPE)
    ang = positions.astype(F32)[..., None] * freqs
    cos2 = jnp.tile(jnp.cos(ang), (1, 1, 2)).reshape(m, MLA_ROPE)
    sin2 = jnp.tile(jnp.sin(ang), (1, 1, 2)).reshape(m, MLA_ROPE)
    z64 = jnp.zeros((m, 64), F32)
    scale = LOG2E * (MLA_NOPE + MLA_ROPE) ** -0.5
    tq = scale * jnp.concatenate([jnp.ones((m, 64), F32), cos2, sin2], axis=1)
    tc = jnp.concatenate([z64, cos2, cos2], axis=1)
    ts = jnp.concatenate([z64, sin2, sin2], axis=1)
    vone_even = _one_hot_row(MLA_HEADS * LANES, [LANES * h + (64 if h % 2 == 0 else 0) for h in range(MLA_HEADS)])
    vone_odd = _one_hot_row(2 * DSA_KV_HEADS * LANES,
                            [LANES * g + 64 for g in range(DSA_KV_HEADS)]
                            + [512 + LANES * g for g in range(DSA_KV_HEADS)])
    low = (jnp.arange(TK)[None, :] < jnp.arange(TK)[:, None]).astype(BF16)

    x2 = x.reshape(m, d)
    for l in range(DEPTH):
        j = l // 2
        if l % 2 == 0:
            w, wq, wkv = _even_weights(e_w_in[j], e_w_uq[j], e_w_ukv[j])
            xr, gr, q, k, v = _even_in(x2, row(e_norm[j]), w, row(e_q_norm[j]), wq, row(e_kv_norm[j]), wkv,
                                       tq, tc, ts, vone_even, b, s)
            cl = -LRU_C * jax.nn.softplus(-e_lambda[j].astype(F32))
            y_lru = _lru(xr.reshape(b, s, -1), gr.reshape(b, s, -1), e_conv_w[j].astype(F32),
                         row(e_conv_b[j]), _block_diag(e_ga_w[j]).astype(BF16), row(e_ga_b[j]),
                         _block_diag(e_gx_w[j]).astype(BF16), row(e_gx_b[j]), row(cl))
            y_mla = _mla(q, k, v)
            wo = e_w_out[j].astype(BF16)
            x2 = _out_proj(x2, [(y_lru.reshape(m, -1), wo[:LRU_WIDTH]), (y_mla.reshape(m, -1), wo[LRU_WIDTH:])])
        else:
            w, wwt = _odd_weights(o_w_in[j])
            kig = jnp.concatenate([row(o_idx_k_norm[j]), jnp.zeros((1, 128), F32), row(o_idx_k_norm[j])], axis=1)
            q, k, v, qi, ki, wit = _odd_in(x2, row(o_norm[j]), w, wwt, kig, vone_odd, b, s)
            y = _dsa(q, k, v, qi, ki, wit, low)
            x2 = _out_proj(x2, [(y.reshape(m, -1), o_w_out[j].astype(BF16))])
        x2 = _mlp(x2, row(m_norm[l]), m_w1[l].astype(BF16), m_w2[l].astype(BF16), row(final_norm),
                  final=(l == DEPTH - 1))
    return x2.reshape(b, s, d)
```

```python
import functools
import math

import jax
import jax.numpy as jnp
from jax import lax
from jax.experimental import pallas as pl
from jax.experimental.pallas import tpu as pltpu

F32 = jnp.float32
BF16 = jnp.bfloat16
I32 = jnp.int32

D_MODEL = 1024
DEPTH = 4
CHUNK = 64
EPS = 1e-6

LRU_WIDTH = D_MODEL // 2
LRU_BLOCKS = 8
CONV_W = 4
LRU_C = 8.0

MLA_HEADS = 8
MLA_NOPE = 64
MLA_ROPE = 32
MLA_V = 64
MLA_Q_RANK = D_MODEL // 4
MLA_KV_RANK = D_MODEL // 8
ROPE_THETA = 10000.0

DSA_HEADS = 16
DSA_KV_HEADS = 4
DSA_HD = 64
IDX_HEADS = 8
IDX_HD = 64
TOPK_MAX = 256

D_FF = 4 * D_MODEL

LANES = 128
SUBLANES = 8
VMEM_LIMIT = 56 * 1024 * 1024
LOG2E = 1.4426950408889634
NEG = -1e30

TM = 512
TQ = 256
TK = 256
LRU_T = 256
MLP_TM = 1024
MLP_TF = 1024
NSLAB = 8


def _params(sem):
    return pltpu.CompilerParams(dimension_semantics=sem, vmem_limit_bytes=VMEM_LIMIT)


def _rms(x, g):
    ms = jnp.mean(x * x, axis=-1, keepdims=True)
    return (x * lax.rsqrt(ms + EPS)) * g


def _dot(a, b):
    return jnp.dot(a, b, preferred_element_type=F32)


def _dot_nt(a, b):
    return lax.dot_general(a, b, (((1,), (1,)), ((), ())), preferred_element_type=F32)


def _slab_spec(nslab, rows, seq):
    nt = seq // rows
    return pl.BlockSpec((None, nslab, rows, LANES), lambda i: (i // nt, 0, i % nt, 0))


def _even_in_kernel(x_ref, g_ref, w_ref, qn_ref, wq_ref, kvn_ref, wkv_ref, tq_ref, tc_ref, ts_ref,
                    vone_ref, xr_ref, gr_ref, q_ref, k_ref, v_ref):
    h = _rms(x_ref[...], g_ref[...]).astype(BF16)
    xr_ref[...] = _dot(h, w_ref[:, 0:512])
    gr_ref[...] = _dot(h, w_ref[:, 512:1024])
    lat = _dot(h, w_ref[:, 1024:1664])
    krope = lat[:, 384:512] * tc_ref[...] + lat[:, 512:640] * ts_ref[...]
    cqn = _rms(lat[:, 0:256], qn_ref[...]).astype(BF16)
    ckvn = _rms(lat[:, 256:384], kvn_ref[...]).astype(BF16)
    tq = tq_ref[...]
    qs = _dot(cqn, wq_ref[...])
    ks = _dot(ckvn, wkv_ref[:, 0:1024])
    vs = _dot(ckvn, wkv_ref[:, 1024:2048]) + vone_ref[...]
    for hd in range(MLA_HEADS):
        sl = slice(LANES * hd, LANES * (hd + 1))
        q_ref[hd] = (qs[:, sl] * tq).astype(BF16)
        k_ref[hd] = (ks[:, sl] + krope).astype(BF16)
        v_ref[hd] = vs[:, sl].astype(BF16)


def _even_in(x2, g, w, qn, wq, kvn, wkv, tq, tc, ts, vone, b, s):
    m = x2.shape[0]
    row = lambda n: pl.BlockSpec((TM, n), lambda i: (i, 0))
    full = lambda a: pl.BlockSpec(a.shape, lambda i: (0, 0))
    slab = _slab_spec(NSLAB, TM, s)
    sds = jax.ShapeDtypeStruct
    return pl.pallas_call(
        _even_in_kernel,
        grid=(m // TM,),
        in_specs=[row(D_MODEL), full(g), full(w), full(qn), full(wq), full(kvn), full(wkv),
                  row(LANES), row(LANES), row(LANES), full(vone)],
        out_specs=[row(512), row(512), slab, slab, slab],
        out_shape=[sds((m, 512), F32), sds((m, 512), F32), sds((b, NSLAB, s, LANES), BF16),
                   sds((b, NSLAB, s, LANES), BF16), sds((b, NSLAB, s, LANES), BF16)],
        compiler_params=_params(("parallel",)),
        name="even_in",
    )(x2, g, w, qn, wq, kvn, wkv, tq, tc, ts, vone)


def _gelu_tanh(x):
    c = math.sqrt(2.0 / math.pi)
    return 0.5 * x * (1.0 + jnp.tanh(c * (x + 0.044715 * (x * x * x))))


def _lru_kernel(xr_ref, gr_ref, cw_ref, cb_ref, wa_ref, ba_ref, wx_ref, bx_ref, cl_ref, y_ref,
                xbuf, hc):
    t = LRU_T

    @pl.when(pl.program_id(1) == 0)
    def _():
        xbuf[0:8, :] = jnp.zeros((8, LRU_WIDTH), F32)
        hc[...] = jnp.zeros((1, LRU_WIDTH), F32)

    x = xr_ref[...]
    xbuf[8:8 + t, :] = x
    xc = cb_ref[...] + cw_ref[3:4, :] * x
    for j in range(CONV_W - 1):
        xc = xc + cw_ref[j:j + 1, :] * xbuf[5 + j:5 + j + t, :]
    xbuf[0:8, :] = x[t - 8:t, :]

    xcb = xc.astype(BF16)
    r = jax.nn.sigmoid(_dot(xcb, wa_ref[...]) + ba_ref[...])
    i = jax.nn.sigmoid(_dot(xcb, wx_ref[...]) + bx_ref[...])
    log_a = cl_ref[...] * r
    a = jnp.exp(log_a)
    th = jnp.tanh(log_a)
    b = jnp.sqrt((-2.0 * th) / (1.0 - th)) * (i * xc)

    rows = lax.broadcasted_iota(I32, (t, LRU_WIDTH), 0)
    d = 1
    while d < t:
        a_sh = pltpu.roll(a, d, 0)
        b_sh = pltpu.roll(b, d, 0)
        keep = rows >= d
        b = jnp.where(keep, a * b_sh + b, b)
        a = jnp.where(keep, a * a_sh, a)
        d *= 2
    h = a * hc[...] + b
    hc[...] = h[t - 1:t, :]
    y_ref[...] = (h * _gelu_tanh(gr_ref[...])).astype(BF16)


def _lru(xr, gr, cw, cb, wa, ba, wx, bx, cl):
    b, s, w = xr.shape
    blk = pl.BlockSpec((None, LRU_T, w), lambda i, j: (i, j, 0))
    full = lambda a: pl.BlockSpec(a.shape, lambda i, j: (0, 0))
    return pl.pallas_call(
        _lru_kernel,
        grid=(b, s // LRU_T),
        in_specs=[blk, blk, full(cw), full(cb), full(wa), full(ba), full(wx), full(bx), full(cl)],
        out_specs=blk,
        out_shape=jax.ShapeDtypeStruct((b, s, w), BF16),
        scratch_shapes=[pltpu.VMEM((LRU_T + 8, w), F32), pltpu.VMEM((1, w), F32)],
        compiler_params=_params(("parallel", "arbitrary")),
        name="rg_lru",
    )(xr, gr, cw, cb, wa, ba, wx, bx, cl)


def _key_rows(kb):
    return pl.ds(pl.multiple_of(kb * TK, TK), TK)


def _chunk_of(pos):
    return pos >> (CHUNK.bit_length() - 1)


def _admissible(c, kb):
    rc = _chunk_of(lax.broadcasted_iota(I32, (TQ, 1), 0) + c * TQ)
    cc = _chunk_of(lax.broadcasted_iota(I32, (1, TK), 1) + kb * TK)
    return cc <= rc


def _admissible_t(c, kb):
    kc = _chunk_of(lax.broadcasted_iota(I32, (TK, 1), 0) + kb * TK)
    qc = _chunk_of(lax.broadcasted_iota(I32, (1, TQ), 1) + c * TQ)
    return kc <= qc


def _flash_reset(m_scr, acc_scr):
    m_scr[...] = jnp.full(m_scr.shape, -jnp.inf, F32)
    acc_scr[...] = jnp.zeros(acc_scr.shape, F32)


def _flash_step(q, k, v, bias, m_ref, acc_ref):
    s = _dot_nt(q, k)
    if bias is not None:
        s = s + bias
    m_old = m_ref[...]
    m_new = jnp.maximum(m_old, jnp.max(s, axis=1, keepdims=True))
    p = jnp.exp2(s - jnp.tile(m_new, (1, s.shape[1] // LANES))).astype(BF16)
    acc_ref[...] = jnp.exp2(m_old - m_new) * acc_ref[...] + _dot(p, v)
    m_ref[...] = m_new


def _pair_slab(acc_e, acc_o):
    lane = lax.broadcasted_iota(I32, acc_e.shape, 1)
    oe = acc_e * (1.0 / acc_e[:, 64:65])
    oo = acc_o * (1.0 / acc_o[:, 0:1])
    return jnp.where(lane < 64, oe, oo).astype(BF16)


def _mla_kernel(q_ref, k_ref, v_ref, o_ref, m_scr, acc_scr):
    c = pl.program_id(1)
    kd = (c * TQ) // TK
    _flash_reset(m_scr, acc_scr)

    def tile(kb, bias):
        rows = _key_rows(kb)
        for hd in range(MLA_HEADS):
            _flash_step(q_ref[hd], k_ref[hd, rows, :], v_ref[hd, rows, :], bias,
                        m_scr.at[hd], acc_scr.at[hd])

    def past_tile(kb, carry):
        tile(kb, None)
        return carry

    lax.fori_loop(0, kd, past_tile, 0)
    tile(kd, jnp.where(_admissible(c, kd), 0.0, NEG).astype(F32))
    for p in range(MLA_HEADS // 2):
        o_ref[:, LANES * p:LANES * (p + 1)] = _pair_slab(acc_scr[2 * p], acc_scr[2 * p + 1])


def _mla(q, k, v):
    b, _, s, _ = q.shape
    assert TQ % TK == 0 or TK % TQ == 0
    return pl.pallas_call(
        _mla_kernel,
        grid=(b, s // TQ),
        in_specs=[pl.BlockSpec((None, NSLAB, TQ, LANES), lambda i, j: (i, 0, j, 0)),
                  pl.BlockSpec((None, NSLAB, s, LANES), lambda i, j: (i, 0, 0, 0)),
                  pl.BlockSpec((None, NSLAB, s, LANES), lambda i, j: (i, 0, 0, 0))],
        out_specs=pl.BlockSpec((None, TQ, 512), lambda i, j: (i, j, 0)),
        out_shape=jax.ShapeDtypeStruct((b, s, 512), BF16),
        scratch_shapes=[pltpu.VMEM((MLA_HEADS, TQ, LANES), F32), pltpu.VMEM((MLA_HEADS, TQ, LANES), F32)],
        compiler_params=_params(("parallel", "arbitrary")),
        name="mla_attn",
    )(q, k, v)


def _odd_in_kernel(x_ref, g_ref, w_ref, wwt_ref, kig_ref, vone_ref,
                   q_ref, k_ref, v_ref, qi_ref, ki_ref, wit_ref):
    h = _rms(x_ref[...], g_ref[...]).astype(BF16)
    lane_slab = lambda a, j: a[:, LANES * j:LANES * (j + 1)]
    for c in range(2):
        qc = _dot(h, w_ref[:, 512 * c:512 * (c + 1)]) * LOG2E
        for j in range(4):
            q_ref[4 * c + j] = lane_slab(qc, j).astype(BF16)
    kc = _dot(h, w_ref[:, 1024:1536])
    vc = _dot(h, w_ref[:, 1536:2048]) + vone_ref[...]
    ic = _dot(h, w_ref[:, 2048:2560])
    for j in range(4):
        k_ref[j] = lane_slab(kc, j).astype(BF16)
        v_ref[j] = lane_slab(vc, j).astype(BF16)
        qi_ref[j] = lane_slab(ic, j).astype(BF16)
    kiab = _dot(h, w_ref[:, 2560:2816])
    kia = kiab[:, 0:LANES]
    kib = kiab[:, LANES:2 * LANES]
    inv = lax.rsqrt(jnp.sum(kia * kia, axis=-1, keepdims=True) * (1.0 / IDX_HD) + EPS)
    ki_ref[0] = ((kia * inv) * kig_ref[:, 0:LANES]).astype(BF16)
    ki_ref[1] = ((kib * inv) * kig_ref[:, LANES:2 * LANES]).astype(BF16)
    wit = _dot_nt(wwt_ref[...], h)[0:IDX_HEADS, :]
    wit_ref[...] = (wit * (IDX_HEADS ** -0.5)) * (IDX_HD ** -0.5)


def _odd_in(x2, g, w, wwt, kig, vone, b, s):
    m = x2.shape[0]
    row = lambda n: pl.BlockSpec((TM, n), lambda i: (i, 0))
    full = lambda a: pl.BlockSpec(a.shape, lambda i: (0, 0))
    sds = jax.ShapeDtypeStruct
    slab = lambda n: sds((b, n, s, LANES), BF16)
    return pl.pallas_call(
        _odd_in_kernel,
        grid=(m // TM,),
        in_specs=[row(D_MODEL), full(g), full(w), full(wwt), full(kig), full(vone)],
        out_specs=[_slab_spec(NSLAB, TM, s), _slab_spec(4, TM, s), _slab_spec(4, TM, s),
                   _slab_spec(4, TM, s), _slab_spec(2, TM, s),
                   pl.BlockSpec((IDX_HEADS, TM), lambda i: (0, i))],
        out_shape=[slab(NSLAB), slab(4), slab(4), slab(4), slab(2), sds((IDX_HEADS, m), F32)],
        compiler_params=_params(("parallel",)),
        name="odd_in",
    )(x2, g, w, wwt, kig, vone)


def _order_key(sc):
    bits = pltpu.bitcast(sc, I32)
    key = jnp.where(bits < 0, bits ^ jnp.int32(0x7FFFFFFF), bits)
    return jnp.where(sc == 0.0, 0, key)


def _dsa_kernel(q_ref, k_ref, v_ref, qi_ref, ki_ref, wit_ref, low_ref, o_ref,
                key_scr, bias_scr, q4_scr, m_scr, acc_scr, *, topk):
    c = pl.program_id(1)
    ntile = (c * TQ) // TK + 1
    part = 4 * SUBLANES

    def score_tile(kb, carry):
        rows = _key_rows(kb)
        score = jnp.zeros((TK, TQ), F32)
        for p in range(IDX_HEADS // 2):
            slab = qi_ref[p]
            for e in range(2):
                hd = 2 * p + e
                z = jnp.maximum(_dot_nt(ki_ref[e, rows, :], slab), 0.0)
                score = score + wit_ref[hd:hd + 1, :] * z
        score = jnp.where(_admissible_t(c, kb), score, -jnp.inf)
        key_scr[kb] = _order_key(score)
        return carry

    lax.fori_loop(0, ntile, score_tile, 0)

    def count(pred):
        def tile(kb, acc):
            hit = jnp.where(pred(key_scr[kb]), 1.0, 0.0)
            return acc + hit.reshape(TK // part, part, TQ).sum(axis=0)
        acc = lax.fori_loop(0, ntile, tile, jnp.zeros((part, TQ), F32))
        return jnp.sum(acc, axis=0, keepdims=True)

    def step(i, t):
        cand = t + jnp.left_shift(jnp.int32(1), 31 - i)
        return jnp.where(count(lambda k: k >= cand) >= float(topk), cand, t)

    t = lax.fori_loop(0, 32, step, jnp.full((1, TQ), -2 ** 31, I32))

    need = float(topk) - count(lambda k: k > t)

    def bias_tile(kb, ties_before):
        key = key_scr[kb]
        eq = jnp.where(key == t, 1.0, 0.0)
        prefix = _dot(low_ref[...], eq.astype(BF16)) + ties_before
        sel = jnp.where(key > t, 1.0, jnp.where(prefix < need, eq, 0.0))
        sel = jnp.where(_admissible_t(c, kb), sel, 0.0)
        bias_scr[kb] = jnp.where(sel > 0.0, 0.0, NEG).T
        return ties_before + jnp.sum(eq, axis=0, keepdims=True)

    lax.fori_loop(0, ntile, bias_tile, jnp.zeros((1, TQ), F32))

    rep = DSA_HEADS // DSA_KV_HEADS
    lane = lax.broadcasted_iota(I32, (TQ, LANES), 1)
    for gp in range(NSLAB):
        slab = q_ref[gp].astype(F32)
        q4_scr[gp // 2, pl.ds((2 * (gp % 2)) * TQ, TQ), :] = jnp.where(lane < 64, slab, 0.0).astype(BF16)
        q4_scr[gp // 2, pl.ds((2 * (gp % 2) + 1) * TQ, TQ), :] = jnp.where(
            lane < 64, pltpu.roll(slab, 64, 1), 0.0).astype(BF16)
    _flash_reset(m_scr, acc_scr)

    def attn_tile(kb, carry):
        rows = _key_rows(kb)
        bias = bias_scr[kb]
        bias4 = jnp.concatenate([bias] * rep, axis=0)
        for g in range(DSA_KV_HEADS):
            _flash_step(q4_scr[g], k_ref[g, rows, :], v_ref[g, rows, :], bias4, m_scr.at[g], acc_scr.at[g])
        return carry

    lax.fori_loop(0, ntile, attn_tile, 0)
    for gp in range(NSLAB):
        g, r0 = gp // 2, 2 * (gp % 2)
        acc_e = acc_scr[g, r0 * TQ:(r0 + 1) * TQ, :]
        acc_o = acc_scr[g, (r0 + 1) * TQ:(r0 + 2) * TQ, :]
        oe = acc_e * (1.0 / acc_e[:, 64:65])
        oo = pltpu.roll(acc_o * (1.0 / acc_o[:, 64:65]), 64, 1)
        o_ref[:, LANES * gp:LANES * (gp + 1)] = jnp.where(lane < 64, oe, oo).astype(BF16)


def _dsa(q, k, v, qi, ki, wit, low):
    b, _, s, _ = q.shape
    topk = min(TOPK_MAX, s // 4)
    assert topk <= TK and s % TK == 0 and TK % TQ == 0
    qblk = lambda n: pl.BlockSpec((None, n, TQ, LANES), lambda i, j: (i, 0, j, 0))
    kblk = lambda n: pl.BlockSpec((None, n, s, LANES), lambda i, j: (i, 0, 0, 0))
    nkb, nq = s // TK, s // TQ
    rep = DSA_HEADS // DSA_KV_HEADS
    return pl.pallas_call(
        functools.partial(_dsa_kernel, topk=topk),
        grid=(b, nq),
        in_specs=[qblk(NSLAB), kblk(DSA_KV_HEADS), kblk(DSA_KV_HEADS), qblk(4), kblk(2),
                  pl.BlockSpec((IDX_HEADS, TQ), lambda i, j: (0, i * nq + j)),
                  pl.BlockSpec((TK, TK), lambda i, j: (0, 0))],
        out_specs=pl.BlockSpec((None, TQ, 1024), lambda i, j: (i, j, 0)),
        out_shape=jax.ShapeDtypeStruct((b, s, 1024), BF16),
        scratch_shapes=[pltpu.VMEM((nkb, TK, TQ), I32), pltpu.VMEM((nkb, TQ, TK), F32),
                        pltpu.VMEM((DSA_KV_HEADS, rep * TQ, LANES), BF16),
                        pltpu.VMEM((DSA_KV_HEADS, rep * TQ, LANES), F32),
                        pltpu.VMEM((DSA_KV_HEADS, rep * TQ, LANES), F32)],
        compiler_params=_params(("parallel", "arbitrary")),
        name="dsa_attn",
    )(q, k, v, qi, ki, wit, low)


def _out_kernel(*refs):
    x_ref, o_ref = refs[0], refs[-1]
    acc = x_ref[...]
    for a_ref, w_ref in zip(refs[1:-1:2], refs[2:-1:2]):
        acc = acc + _dot(a_ref[...], w_ref[...])
    o_ref[...] = acc


def _out_proj(x2, pairs):
    m = x2.shape[0]
    row = lambda n: pl.BlockSpec((TM, n), lambda i: (i, 0))
    full = lambda a: pl.BlockSpec(a.shape, lambda i: (0, 0))
    specs, args = [row(D_MODEL)], [x2]
    for a, w in pairs:
        specs += [row(a.shape[1]), full(w)]
        args += [a, w]
    return pl.pallas_call(
        _out_kernel,
        grid=(m // TM,),
        in_specs=specs,
        out_specs=row(D_MODEL),
        out_shape=jax.ShapeDtypeStruct((m, D_MODEL), F32),
        compiler_params=_params(("parallel",)),
        name="out_proj",
    )(*args)


def _mlp_kernel(x_ref, g_ref, w1_ref, w2_ref, fg_ref, o_ref, h_scr, acc, *, final):
    j = pl.program_id(1)

    @pl.when(j == 0)
    def _():
        h_scr[...] = _rms(x_ref[...], g_ref[...]).astype(BF16)
        acc[...] = jnp.zeros_like(acc)

    a = jnp.maximum(_dot(h_scr[...], w1_ref[...]), 0.0)
    acc[...] += _dot((a * a).astype(BF16), w2_ref[...])

    @pl.when(j == pl.num_programs(1) - 1)
    def _():
        y = x_ref[...] + acc[...]
        o_ref[...] = _rms(y, fg_ref[...]) if final else y


def _mlp(x2, g, w1, w2, fg, final):
    m = x2.shape[0]
    return pl.pallas_call(
        functools.partial(_mlp_kernel, final=final),
        grid=(m // MLP_TM, D_FF // MLP_TF),
        in_specs=[pl.BlockSpec((MLP_TM, D_MODEL), lambda i, j: (i, 0)),
                  pl.BlockSpec((1, D_MODEL), lambda i, j: (0, 0)),
                  pl.BlockSpec((D_MODEL, MLP_TF), lambda i, j: (0, j)),
                  pl.BlockSpec((MLP_TF, D_MODEL), lambda i, j: (j, 0)),
                  pl.BlockSpec((1, D_MODEL), lambda i, j: (0, 0))],
        out_specs=pl.BlockSpec((MLP_TM, D_MODEL), lambda i, j: (i, 0)),
        out_shape=jax.ShapeDtypeStruct((m, D_MODEL), F32),
        scratch_shapes=[pltpu.VMEM((MLP_TM, D_MODEL), BF16), pltpu.VMEM((MLP_TM, D_MODEL), F32)],
        compiler_params=_params(("parallel", "arbitrary")),
        name="mlp",
    )(x2, g, w1, w2, fg)


def _rot_cols(w):
    half = w.shape[-1] // 2
    return jnp.concatenate([-w[..., half:], w[..., :half]], axis=-1)


def _one_hot_row(n, idx):
    return jnp.zeros((1, n), F32).at[0, jnp.asarray(idx)].set(1.0)


def _even_weights(w_in, w_uq, w_ukv):
    d = w_in.shape[0]
    wkr = w_in[:, 1408:1440]
    z64 = jnp.zeros((d, 64), F32)
    w = jnp.concatenate([w_in[:, :1408], z64, wkr, wkr, z64, _rot_cols(wkr), _rot_cols(wkr)], axis=1)
    uq = w_uq.reshape(MLA_Q_RANK, MLA_HEADS, MLA_NOPE + MLA_ROPE)
    rope_w = uq[..., MLA_NOPE:]
    wq = jnp.concatenate([uq[..., :MLA_NOPE], rope_w, _rot_cols(rope_w)], axis=-1)
    wq = wq.reshape(MLA_Q_RANK, MLA_HEADS * LANES)
    ukv = w_ukv.reshape(MLA_KV_RANK, MLA_HEADS, MLA_NOPE + MLA_V)
    zk = jnp.zeros((MLA_KV_RANK, MLA_HEADS, 64), F32)
    wk = jnp.concatenate([ukv[..., :MLA_NOPE], zk], axis=-1).reshape(MLA_KV_RANK, MLA_HEADS * LANES)
    vv = ukv[..., MLA_NOPE:].reshape(MLA_KV_RANK, MLA_HEADS // 2, 2, MLA_V)
    zv = jnp.zeros((MLA_KV_RANK, MLA_HEADS // 2, MLA_V), F32)
    v_even = jnp.concatenate([vv[:, :, 0], zv], axis=-1)
    v_odd = jnp.concatenate([zv, vv[:, :, 1]], axis=-1)
    wv = jnp.stack([v_even, v_odd], axis=2).reshape(MLA_KV_RANK, MLA_HEADS * LANES)
    wkv = jnp.concatenate([wk, wv], axis=1)
    return w.astype(BF16), wq.astype(BF16), wkv.astype(BF16)


def _odd_weights(w_in):
    d = w_in.shape[0]
    wq = w_in[:, :1024] * (DSA_HD ** -0.5)
    wk = w_in[:, 1024:1280].reshape(d, DSA_KV_HEADS, DSA_HD)
    wv = w_in[:, 1280:1536].reshape(d, DSA_KV_HEADS, DSA_HD)
    z = jnp.zeros((d, DSA_KV_HEADS, DSA_HD), F32)
    pad = lambda a: jnp.concatenate([a, z], -1).reshape(d, -1)
    wqi = w_in[:, 1536:2048]
    wki = w_in[:, 2048:2112]
    z64 = jnp.zeros((d, 64), F32)
    w = jnp.concatenate([wq, pad(wk), pad(wv), wqi, wki, z64, z64, wki], axis=1)
    wwt = jnp.concatenate([w_in[:, 2112:2120].T, jnp.zeros((2 * SUBLANES - IDX_HEADS, d), F32)], axis=0)
    return w.astype(BF16), wwt.astype(BF16)


def _block_diag(w):
    n, k, _ = w.shape
    eye = jnp.eye(n, dtype=w.dtype)
    return (w[:, :, None, :] * eye[:, None, :, None]).reshape(n * k, n * k)


def kernel(x, positions, e_norm, e_w_in, e_conv_w, e_conv_b, e_ga_w, e_ga_b, e_gx_w, e_gx_b, e_lambda,
           e_q_norm, e_w_uq, e_kv_norm, e_w_ukv, e_w_out, o_norm, o_w_in, o_idx_k_norm, o_w_out,
           m_norm, m_w1, m_w2, final_norm):
    b, s, d = x.shape
    m = b * s
    row = lambda v: v.reshape(1, -1).astype(F32)

    freqs = ROPE_THETA ** (-jnp.arange(0, MLA_ROPE, 2, dtype=F32) / MLA_ROPE)
    ang = positions.astype(F32)[..., None] * freqs
    cos2 = jnp.tile(jnp.cos(ang), (1, 1, 2)).reshape(m, MLA_ROPE)
    sin2 = jnp.tile(jnp.sin(ang), (1, 1, 2)).reshape(m, MLA_ROPE)
    z64 = jnp.zeros((m, 64), F32)
    scale = LOG2E * (MLA_NOPE + MLA_ROPE) ** -0.5
    tq = scale * jnp.concatenate([jnp.ones((m, 64), F32), cos2, sin2], axis=1)
    tc = jnp.concatenate([z64, cos2, cos2], axis=1)
    ts = jnp.concatenate([z64, sin2, sin2], axis=1)
    vone_even = _one_hot_row(MLA_HEADS * LANES, [LANES * h + (64 if h % 2 == 0 else 0) for h in range(MLA_HEADS)])
    vone_odd = _one_hot_row(DSA_KV_HEADS * LANES, [LANES * g + 64 for g in range(DSA_KV_HEADS)])
    low = (jnp.arange(TK)[None, :] < jnp.arange(TK)[:, None]).astype(BF16)

    x2 = x.reshape(m, d)
    for l in range(DEPTH):
        j = l // 2
        if l % 2 == 0:
            w, wq, wkv = _even_weights(e_w_in[j], e_w_uq[j], e_w_ukv[j])
            xr, gr, q, k, v = _even_in(x2, row(e_norm[j]), w, row(e_q_norm[j]), wq, row(e_kv_norm[j]), wkv,
                                       tq, tc, ts, vone_even, b, s)
            cl = -LRU_C * jax.nn.softplus(-e_lambda[j].astype(F32))
            y_lru = _lru(xr.reshape(b, s, -1), gr.reshape(b, s, -1), e_conv_w[j].astype(F32),
                         row(e_conv_b[j]), _block_diag(e_ga_w[j]).astype(BF16), row(e_ga_b[j]),
                         _block_diag(e_gx_w[j]).astype(BF16), row(e_gx_b[j]), row(cl))
            y_mla = _mla(q, k, v)
            wo = e_w_out[j].astype(BF16)
            x2 = _out_proj(x2, [(y_lru.reshape(m, -1), wo[:LRU_WIDTH]), (y_mla.reshape(m, -1), wo[LRU_WIDTH:])])
        else:
            w, wwt = _odd_weights(o_w_in[j])
            kig = jnp.concatenate([row(o_idx_k_norm[j]), jnp.zeros((1, 128), F32), row(o_idx_k_norm[j])], axis=1)
            q, k, v, qi, ki, wit = _odd_in(x2, row(o_norm[j]), w, wwt, kig, vone_odd, b, s)
            y = _dsa(q, k, v, qi, ki, wit, low)
            x2 = _out_proj(x2, [(y.reshape(m, -1), o_w_out[j].astype(BF16))])
        x2 = _mlp(x2, row(m_norm[l]), m_w1[l].astype(BF16), m_w2[l].astype(BF16), row(final_norm),
                  final=(l == DEPTH - 1))
    return x2.reshape(b, s, d)
```

```python
import functools
import math

import jax
import jax.numpy as jnp
from jax import lax
from jax.experimental import pallas as pl
from jax.experimental.pallas import tpu as pltpu

F32 = jnp.float32
BF16 = jnp.bfloat16
I32 = jnp.int32

D_MODEL = 1024
DEPTH = 4
CHUNK = 64
EPS = 1e-6

LRU_WIDTH = D_MODEL // 2
LRU_BLOCKS = 8
CONV_W = 4
LRU_C = 8.0

MLA_HEADS = 8
MLA_NOPE = 64
MLA_ROPE = 32
MLA_V = 64
MLA_Q_RANK = D_MODEL // 4
MLA_KV_RANK = D_MODEL // 8
ROPE_THETA = 10000.0

DSA_HEADS = 16
DSA_KV_HEADS = 4
DSA_HD = 64
IDX_HEADS = 8
IDX_HD = 64
TOPK_MAX = 256

D_FF = 4 * D_MODEL

LANES = 128
SUBLANES = 8
VMEM_LIMIT = 56 * 1024 * 1024
LOG2E = 1.4426950408889634
NEG = -1e30

TM = 512
TQ = 256
TK = 256
LRU_T = 256
MLP_TM = 1024
MLP_TF = 1024
NSLAB = 8


def _params(sem):
    return pltpu.CompilerParams(dimension_semantics=sem, vmem_limit_bytes=VMEM_LIMIT)


def _rms(x, g):
    ms = jnp.mean(x * x, axis=-1, keepdims=True)
    return (x * lax.rsqrt(ms + EPS)) * g


def _dot(a, b):
    return jnp.dot(a, b, preferred_element_type=F32)


def _dot_nt(a, b):
    return lax.dot_general(a, b, (((1,), (1,)), ((), ())), preferred_element_type=F32)


def _slab_spec(nslab, rows, seq):
    nt = seq // rows
    return pl.BlockSpec((None, nslab, rows, LANES), lambda i: (i // nt, 0, i % nt, 0))


MLA_QW = 2 * LANES
MLA_ONE = LANES + 64


def _even_in_kernel(x_ref, g_ref, w_ref, qn_ref, wq_ref, kvn_ref, tq_ref, tc_ref, ts_ref, one_ref,
                    xr_ref, gr_ref, q_ref, kv_ref, *, qscale):
    h = _rms(x_ref[...], g_ref[...]).astype(BF16)
    xr_ref[...] = _dot(h, w_ref[:, 0:512])
    gr_ref[...] = _dot(h, w_ref[:, 512:1024])
    lat = _dot(h, w_ref[:, 1024:1664])
    krope = lat[:, 384:512] * tc_ref[...] + lat[:, 512:640] * ts_ref[...]
    kv_ref[:, 0:LANES] = _rms(lat[:, 256:384], kvn_ref[...]).astype(BF16)
    kv_ref[:, LANES:2 * LANES] = (krope + one_ref[...]).astype(BF16)
    cqn = _rms(lat[:, 0:256], qn_ref[...]).astype(BF16)
    tq = tq_ref[...]
    for c in range(MLA_HEADS // 2):
        qs = _dot(cqn, wq_ref[:, 2 * MLA_QW * c:2 * MLA_QW * (c + 1)])
        for e in range(2):
            o = MLA_QW * e
            q_ref[2 * c + e, :, 0:LANES] = (qs[:, o:o + LANES] * qscale).astype(BF16)
            q_ref[2 * c + e, :, LANES:MLA_QW] = (qs[:, o + LANES:o + MLA_QW] * tq).astype(BF16)


def _even_in(x2, g, w, qn, wq, kvn, tq, tc, ts, one, b, s, qscale):
    m = x2.shape[0]
    nt = s // TM
    row = lambda n: pl.BlockSpec((TM, n), lambda i: (i, 0))
    full = lambda a: pl.BlockSpec(a.shape, lambda i: (0, 0))
    sds = jax.ShapeDtypeStruct
    return pl.pallas_call(
        functools.partial(_even_in_kernel, qscale=qscale),
        grid=(m // TM,),
        in_specs=[row(D_MODEL), full(g), full(w), full(qn), full(wq), full(kvn),
                  row(LANES), row(LANES), row(LANES), full(one)],
        out_specs=[row(512), row(512),
                   pl.BlockSpec((None, MLA_HEADS, TM, MLA_QW), lambda i: (i // nt, 0, i % nt, 0)),
                   row(MLA_QW)],
        out_shape=[sds((m, 512), F32), sds((m, 512), F32), sds((b, MLA_HEADS, s, MLA_QW), BF16),
                   sds((m, MLA_QW), BF16)],
        compiler_params=_params(("parallel",)),
        name="even_in",
    )(x2, g, w, qn, wq, kvn, tq, tc, ts, one)


def _gelu_tanh(x):
    c = math.sqrt(2.0 / math.pi)
    return 0.5 * x * (1.0 + jnp.tanh(c * (x + 0.044715 * (x * x * x))))


def _lru_kernel(xr_ref, gr_ref, cw_ref, cb_ref, wa_ref, ba_ref, wx_ref, bx_ref, cl_ref, y_ref,
                xbuf, hc):
    t = LRU_T

    @pl.when(pl.program_id(1) == 0)
    def _():
        xbuf[0:8, :] = jnp.zeros((8, LRU_WIDTH), F32)
        hc[...] = jnp.zeros((1, LRU_WIDTH), F32)

    x = xr_ref[...]
    xbuf[8:8 + t, :] = x
    xc = cb_ref[...] + cw_ref[3:4, :] * x
    for j in range(CONV_W - 1):
        xc = xc + cw_ref[j:j + 1, :] * xbuf[5 + j:5 + j + t, :]
    xbuf[0:8, :] = x[t - 8:t, :]

    xcb = xc.astype(BF16)
    r = jax.nn.sigmoid(_dot(xcb, wa_ref[...]) + ba_ref[...])
    i = jax.nn.sigmoid(_dot(xcb, wx_ref[...]) + bx_ref[...])
    log_a = cl_ref[...] * r
    a = jnp.exp(log_a)
    th = jnp.tanh(log_a)
    b = jnp.sqrt((-2.0 * th) / (1.0 - th)) * (i * xc)

    rows = lax.broadcasted_iota(I32, (t, LRU_WIDTH), 0)
    d = 1
    while d < t:
        a_sh = pltpu.roll(a, d, 0)
        b_sh = pltpu.roll(b, d, 0)
        keep = rows >= d
        b = jnp.where(keep, a * b_sh + b, b)
        a = jnp.where(keep, a * a_sh, a)
        d *= 2
    h = a * hc[...] + b
    hc[...] = h[t - 1:t, :]
    y_ref[...] = (h * _gelu_tanh(gr_ref[...])).astype(BF16)


def _lru(xr, gr, cw, cb, wa, ba, wx, bx, cl):
    b, s, w = xr.shape
    blk = pl.BlockSpec((None, LRU_T, w), lambda i, j: (i, j, 0))
    full = lambda a: pl.BlockSpec(a.shape, lambda i, j: (0, 0))
    return pl.pallas_call(
        _lru_kernel,
        grid=(b, s // LRU_T),
        in_specs=[blk, blk, full(cw), full(cb), full(wa), full(ba), full(wx), full(bx), full(cl)],
        out_specs=blk,
        out_shape=jax.ShapeDtypeStruct((b, s, w), BF16),
        scratch_shapes=[pltpu.VMEM((LRU_T + 8, w), F32), pltpu.VMEM((1, w), F32)],
        compiler_params=_params(("parallel", "arbitrary")),
        name="rg_lru",
    )(xr, gr, cw, cb, wa, ba, wx, bx, cl)


def _key_rows(kb):
    return pl.ds(pl.multiple_of(kb * TK, TK), TK)


def _chunk_of(pos):
    return pos >> (CHUNK.bit_length() - 1)


def _admissible(c, kb):
    rc = _chunk_of(lax.broadcasted_iota(I32, (TQ, 1), 0) + c * TQ)
    cc = _chunk_of(lax.broadcasted_iota(I32, (1, TK), 1) + kb * TK)
    return cc <= rc


def _admissible_t(c, kb):
    kc = _chunk_of(lax.broadcasted_iota(I32, (TK, 1), 0) + kb * TK)
    qc = _chunk_of(lax.broadcasted_iota(I32, (1, TQ), 1) + c * TQ)
    return kc <= qc


def _flash_reset(m_scr, acc_scr):
    m_scr[...] = jnp.full(m_scr.shape, -jnp.inf, F32)
    acc_scr[...] = jnp.zeros(acc_scr.shape, F32)


def _flash_step(q, k, v, bias, m_ref, acc_ref):
    s = _dot_nt(q, k)
    if bias is not None:
        s = s + bias
    m_old = m_ref[...]
    m_new = jnp.maximum(m_old, jnp.max(s, axis=1, keepdims=True))
    p = jnp.exp2(s - jnp.tile(m_new, (1, s.shape[1] // LANES))).astype(BF16)
    alpha = jnp.tile(jnp.exp2(m_old - m_new), (1, acc_ref.shape[1] // LANES))
    acc_ref[...] = alpha * acc_ref[...] + _dot(p, v)
    m_ref[...] = m_new


MLA_CHAINS = 2


def _mla_kernel(q_ref, kv_ref, wuv_ref, o_ref, m_scr, acc_scr):
    c = pl.program_id(1)
    kd = (c * TQ) // TK
    per = MLA_HEADS // MLA_CHAINS
    _flash_reset(m_scr, acc_scr)

    def tile(kb, bias):
        kv = kv_ref[_key_rows(kb), :]
        for ch in range(MLA_CHAINS):
            q = q_ref[per * ch:per * (ch + 1)].reshape(per * TQ, MLA_QW)
            _flash_step(q, kv, kv, bias, m_scr.at[ch], acc_scr.at[ch])

    def past_tile(kb, carry):
        tile(kb, None)
        return carry

    lax.fori_loop(0, kd, past_tile, 0)
    dbias = jnp.where(_admissible(c, kd), 0.0, NEG).astype(F32)
    tile(kd, jnp.concatenate([dbias] * per, axis=0))

    def latent(hd):
        rows = slice((hd % per) * TQ, (hd % per + 1) * TQ)
        acc = acc_scr[hd // per, rows, :]
        return (acc[:, 0:LANES] * (1.0 / acc[:, MLA_ONE:MLA_ONE + 1])).astype(BF16)

    for p in range(MLA_HEADS // 2):
        y = _dot(latent(2 * p), wuv_ref[2 * p]) + _dot(latent(2 * p + 1), wuv_ref[2 * p + 1])
        o_ref[:, LANES * p:LANES * (p + 1)] = y.astype(BF16)


def _mla(q, kv, wuv):
    b, _, s, _ = q.shape
    assert TQ == TK
    per = MLA_HEADS // MLA_CHAINS
    return pl.pallas_call(
        _mla_kernel,
        grid=(b, s // TQ),
        in_specs=[pl.BlockSpec((None, MLA_HEADS, TQ, MLA_QW), lambda i, j: (i, 0, j, 0)),
                  pl.BlockSpec((None, s, MLA_QW), lambda i, j: (i, 0, 0)),
                  pl.BlockSpec(wuv.shape, lambda i, j: (0, 0, 0))],
        out_specs=pl.BlockSpec((None, TQ, 512), lambda i, j: (i, j, 0)),
        out_shape=jax.ShapeDtypeStruct((b, s, 512), BF16),
        scratch_shapes=[pltpu.VMEM((MLA_CHAINS, per * TQ, LANES), F32),
                        pltpu.VMEM((MLA_CHAINS, per * TQ, MLA_QW), F32)],
        compiler_params=_params(("parallel", "arbitrary")),
        name="mla_attn",
    )(q, kv, wuv)


def _odd_in_kernel(x_ref, g_ref, w_ref, wwt_ref, kig_ref, vone_ref,
                   q_ref, k_ref, v_ref, qi_ref, ki_ref, wit_ref):
    h = _rms(x_ref[...], g_ref[...]).astype(BF16)
    lane_slab = lambda a, j: a[:, LANES * j:LANES * (j + 1)]
    for c in range(2):
        qc = _dot(h, w_ref[:, 512 * c:512 * (c + 1)]) * LOG2E
        for j in range(4):
            q_ref[4 * c + j] = lane_slab(qc, j).astype(BF16)
    kc = _dot(h, w_ref[:, 1024:1536])
    vc = _dot(h, w_ref[:, 1536:2048]) + vone_ref[...]
    ic = _dot(h, w_ref[:, 2048:2560])
    for j in range(4):
        k_ref[j] = lane_slab(kc, j).astype(BF16)
        v_ref[j] = lane_slab(vc, j).astype(BF16)
        qi_ref[j] = lane_slab(ic, j).astype(BF16)
    kiab = _dot(h, w_ref[:, 2560:2816])
    kia = kiab[:, 0:LANES]
    kib = kiab[:, LANES:2 * LANES]
    inv = lax.rsqrt(jnp.sum(kia * kia, axis=-1, keepdims=True) * (1.0 / IDX_HD) + EPS)
    ki_ref[0] = ((kia * inv) * kig_ref[:, 0:LANES]).astype(BF16)
    ki_ref[1] = ((kib * inv) * kig_ref[:, LANES:2 * LANES]).astype(BF16)
    wit = _dot_nt(wwt_ref[...], h)[0:IDX_HEADS, :]
    wit_ref[...] = (wit * (IDX_HEADS ** -0.5)) * (IDX_HD ** -0.5)


def _odd_in(x2, g, w, wwt, kig, vone, b, s):
    m = x2.shape[0]
    row = lambda n: pl.BlockSpec((TM, n), lambda i: (i, 0))
    full = lambda a: pl.BlockSpec(a.shape, lambda i: (0, 0))
    sds = jax.ShapeDtypeStruct
    slab = lambda n: sds((b, n, s, LANES), BF16)
    return pl.pallas_call(
        _odd_in_kernel,
        grid=(m // TM,),
        in_specs=[row(D_MODEL), full(g), full(w), full(wwt), full(kig), full(vone)],
        out_specs=[_slab_spec(NSLAB, TM, s), _slab_spec(4, TM, s), _slab_spec(4, TM, s),
                   _slab_spec(4, TM, s), _slab_spec(2, TM, s),
                   pl.BlockSpec((IDX_HEADS, TM), lambda i: (0, i))],
        out_shape=[slab(NSLAB), slab(4), slab(4), slab(4), slab(2), sds((IDX_HEADS, m), F32)],
        compiler_params=_params(("parallel",)),
        name="odd_in",
    )(x2, g, w, wwt, kig, vone)


def _order_key(sc):
    bits = pltpu.bitcast(sc, I32)
    key = jnp.where(bits < 0, bits ^ jnp.int32(0x7FFFFFFF), bits)
    return jnp.where(sc == 0.0, 0, key)


def _dsa_kernel(q_ref, k_ref, v_ref, qi_ref, ki_ref, wit_ref, low_ref, o_ref,
                key_scr, bias_scr, q4_scr, m_scr, acc_scr, *, topk):
    c = pl.program_id(1)
    ntile = (c * TQ) // TK + 1
    part = 4 * SUBLANES

    def score_tile(kb, carry):
        rows = _key_rows(kb)
        score = jnp.zeros((TK, TQ), F32)
        for p in range(IDX_HEADS // 2):
            slab = qi_ref[p]
            for e in range(2):
                hd = 2 * p + e
                z = jnp.maximum(_dot_nt(ki_ref[e, rows, :], slab), 0.0)
                score = score + wit_ref[hd:hd + 1, :] * z
        score = jnp.where(_admissible_t(c, kb), score, -jnp.inf)
        key_scr[kb] = _order_key(score)
        return carry

    lax.fori_loop(0, ntile, score_tile, 0)

    def count(pred):
        def tile(kb, acc):
            hit = jnp.where(pred(key_scr[kb]), 1.0, 0.0)
            return acc + hit.reshape(TK // part, part, TQ).sum(axis=0)
        acc = lax.fori_loop(0, ntile, tile, jnp.zeros((part, TQ), F32))
        return jnp.sum(acc, axis=0, keepdims=True)

    def step(i, t):
        cand = t + jnp.left_shift(jnp.int32(1), 31 - i)
        return jnp.where(count(lambda k: k >= cand) >= float(topk), cand, t)

    t = lax.fori_loop(0, 32, step, jnp.full((1, TQ), -2 ** 31, I32))

    need = float(topk) - count(lambda k: k > t)

    def bias_tile(kb, ties_before):
        key = key_scr[kb]
        eq = jnp.where(key == t, 1.0, 0.0)
        prefix = _dot(low_ref[...], eq.astype(BF16)) + ties_before
        sel = jnp.where(key > t, 1.0, jnp.where(prefix < need, eq, 0.0))
        sel = jnp.where(_admissible_t(c, kb), sel, 0.0)
        bias_scr[kb] = jnp.where(sel > 0.0, 0.0, NEG).T
        return ties_before + jnp.sum(eq, axis=0, keepdims=True)

    lax.fori_loop(0, ntile, bias_tile, jnp.zeros((1, TQ), F32))

    rep = DSA_HEADS // DSA_KV_HEADS
    lane = lax.broadcasted_iota(I32, (TQ, LANES), 1)
    for gp in range(NSLAB):
        slab = q_ref[gp].astype(F32)
        q4_scr[gp // 2, pl.ds((2 * (gp % 2)) * TQ, TQ), :] = jnp.where(lane < 64, slab, 0.0).astype(BF16)
        q4_scr[gp // 2, pl.ds((2 * (gp % 2) + 1) * TQ, TQ), :] = jnp.where(
            lane < 64, pltpu.roll(slab, 64, 1), 0.0).astype(BF16)
    _flash_reset(m_scr, acc_scr)

    def attn_tile(kb, carry):
        rows = _key_rows(kb)
        bias = bias_scr[kb]
        bias4 = jnp.concatenate([bias] * rep, axis=0)
        for g in range(DSA_KV_HEADS):
            _flash_step(q4_scr[g], k_ref[g, rows, :], v_ref[g, rows, :], bias4, m_scr.at[g], acc_scr.at[g])
        return carry

    lax.fori_loop(0, ntile, attn_tile, 0)
    for gp in range(NSLAB):
        g, r0 = gp // 2, 2 * (gp % 2)
        acc_e = acc_scr[g, r0 * TQ:(r0 + 1) * TQ, :]
        acc_o = acc_scr[g, (r0 + 1) * TQ:(r0 + 2) * TQ, :]
        oe = acc_e * (1.0 / acc_e[:, 64:65])
        oo = pltpu.roll(acc_o * (1.0 / acc_o[:, 64:65]), 64, 1)
        o_ref[:, LANES * gp:LANES * (gp + 1)] = jnp.where(lane < 64, oe, oo).astype(BF16)


def _dsa(q, k, v, qi, ki, wit, low):
    b, _, s, _ = q.shape
    topk = min(TOPK_MAX, s // 4)
    assert topk <= TK and s % TK == 0 and TK % TQ == 0
    qblk = lambda n: pl.BlockSpec((None, n, TQ, LANES), lambda i, j: (i, 0, j, 0))
    kblk = lambda n: pl.BlockSpec((None, n, s, LANES), lambda i, j: (i, 0, 0, 0))
    nkb, nq = s // TK, s // TQ
    rep = DSA_HEADS // DSA_KV_HEADS
    return pl.pallas_call(
        functools.partial(_dsa_kernel, topk=topk),
        grid=(b, nq),
        in_specs=[qblk(NSLAB), kblk(DSA_KV_HEADS), kblk(DSA_KV_HEADS), qblk(4), kblk(2),
                  pl.BlockSpec((IDX_HEADS, TQ), lambda i, j: (0, i * nq + j)),
                  pl.BlockSpec((TK, TK), lambda i, j: (0, 0))],
        out_specs=pl.BlockSpec((None, TQ, 1024), lambda i, j: (i, j, 0)),
        out_shape=jax.ShapeDtypeStruct((b, s, 1024), BF16),
        scratch_shapes=[pltpu.VMEM((nkb, TK, TQ), I32), pltpu.VMEM((nkb, TQ, TK), F32),
                        pltpu.VMEM((DSA_KV_HEADS, rep * TQ, LANES), BF16),
                        pltpu.VMEM((DSA_KV_HEADS, rep * TQ, LANES), F32),
                        pltpu.VMEM((DSA_KV_HEADS, rep * TQ, LANES), F32)],
        compiler_params=_params(("parallel", "arbitrary")),
        name="dsa_attn",
    )(q, k, v, qi, ki, wit, low)


def _mix_mlp_kernel(*refs, n_mix, final):
    x_ref = refs[0]
    mix = refs[1:1 + 2 * n_mix]
    g_ref, w1_ref, w2_ref, fg_ref, o_ref, x1_scr, h_scr, acc = refs[1 + 2 * n_mix:]
    j = pl.program_id(1)

    @pl.when(j == 0)
    def _():
        x1 = x_ref[...]
        for a_ref, w_ref in zip(mix[0::2], mix[1::2]):
            x1 = x1 + _dot(a_ref[...], w_ref[...])
        x1_scr[...] = x1
        h_scr[...] = _rms(x1, g_ref[...]).astype(BF16)
        acc[...] = jnp.zeros_like(acc)

    a = jnp.maximum(_dot(h_scr[...], w1_ref[...]), 0.0)
    acc[...] += _dot((a * a).astype(BF16), w2_ref[...])

    @pl.when(j == pl.num_programs(1) - 1)
    def _():
        y = x1_scr[...] + acc[...]
        o_ref[...] = _rms(y, fg_ref[...]) if final else y


def _mix_mlp(x2, mix, g, w1, w2, layer, fg, final):
    m = x2.shape[0]
    specs, args = [pl.BlockSpec((MLP_TM, D_MODEL), lambda i, j: (i, 0))], [x2]
    for a, w, (n, rb) in mix:
        k = a.shape[1]
        specs += [pl.BlockSpec((MLP_TM, k), lambda i, j: (i, 0)),
                  pl.BlockSpec((None, k, D_MODEL), lambda i, j, n=n, rb=rb: (n, rb, 0))]
        args += [a, w]
    specs += [pl.BlockSpec((1, D_MODEL), lambda i, j: (0, 0)),
              pl.BlockSpec((None, D_MODEL, MLP_TF), lambda i, j: (layer, 0, j)),
              pl.BlockSpec((None, MLP_TF, D_MODEL), lambda i, j: (layer, j, 0)),
              pl.BlockSpec((1, D_MODEL), lambda i, j: (0, 0))]
    args += [g, w1, w2, fg]
    return pl.pallas_call(
        functools.partial(_mix_mlp_kernel, n_mix=len(mix), final=final),
        grid=(m // MLP_TM, D_FF // MLP_TF),
        in_specs=specs,
        out_specs=pl.BlockSpec((MLP_TM, D_MODEL), lambda i, j: (i, 0)),
        out_shape=jax.ShapeDtypeStruct((m, D_MODEL), F32),
        scratch_shapes=[pltpu.VMEM((MLP_TM, D_MODEL), F32), pltpu.VMEM((MLP_TM, D_MODEL), BF16),
                        pltpu.VMEM((MLP_TM, D_MODEL), F32)],
        compiler_params=_params(("parallel", "arbitrary")),
        name="mix_mlp",
    )(*args)


def _rot_cols(w):
    half = w.shape[-1] // 2
    return jnp.concatenate([-w[..., half:], w[..., :half]], axis=-1)


def _one_hot_row(n, idx):
    return jnp.zeros((1, n), F32).at[0, jnp.asarray(idx)].set(1.0)


def _even_weights(w_in, w_uq, w_ukv):
    d = w_in.shape[0]
    wkr = w_in[:, 1408:1440]
    z64 = jnp.zeros((d, 64), F32)
    w = jnp.concatenate([w_in[:, :1408], wkr, wkr, z64, _rot_cols(wkr), _rot_cols(wkr), z64], axis=1)
    uq = w_uq.reshape(MLA_Q_RANK, MLA_HEADS, MLA_NOPE + MLA_ROPE)
    ukv = w_ukv.reshape(MLA_KV_RANK, MLA_HEADS, MLA_NOPE + MLA_V)
    w_abs = jnp.einsum('rhd,lhd->rhl', uq[..., :MLA_NOPE], ukv[..., :MLA_NOPE],
                       precision=lax.Precision.HIGHEST)
    rope_w = uq[..., MLA_NOPE:]
    zq = jnp.zeros((MLA_Q_RANK, MLA_HEADS, MLA_QW - MLA_KV_RANK - 2 * MLA_ROPE), F32)
    wq = jnp.concatenate([w_abs, rope_w, _rot_cols(rope_w), zq], axis=-1)
    wq = wq.reshape(MLA_Q_RANK, MLA_HEADS * MLA_QW)
    vv = ukv[..., MLA_NOPE:].transpose(1, 0, 2)
    zv = jnp.zeros_like(vv)
    even = (jnp.arange(MLA_HEADS) % 2 == 0)[:, None, None]
    wuv = jnp.where(even, jnp.concatenate([vv, zv], axis=-1), jnp.concatenate([zv, vv], axis=-1))
    return w.astype(BF16), wq.astype(BF16), wuv.astype(BF16)


def _odd_weights(w_in):
    d = w_in.shape[0]
    wq = w_in[:, :1024] * (DSA_HD ** -0.5)
    wk = w_in[:, 1024:1280].reshape(d, DSA_KV_HEADS, DSA_HD)
    wv = w_in[:, 1280:1536].reshape(d, DSA_KV_HEADS, DSA_HD)
    z = jnp.zeros((d, DSA_KV_HEADS, DSA_HD), F32)
    pad = lambda a: jnp.concatenate([a, z], -1).reshape(d, -1)
    wqi = w_in[:, 1536:2048]
    wki = w_in[:, 2048:2112]
    z64 = jnp.zeros((d, 64), F32)
    w = jnp.concatenate([wq, pad(wk), pad(wv), wqi, wki, z64, z64, wki], axis=1)
    wwt = jnp.concatenate([w_in[:, 2112:2120].T, jnp.zeros((2 * SUBLANES - IDX_HEADS, d), F32)], axis=0)
    return w.astype(BF16), wwt.astype(BF16)


def _block_diag(w):
    n, k, _ = w.shape
    eye = jnp.eye(n, dtype=w.dtype)
    return (w[:, :, None, :] * eye[:, None, :, None]).reshape(n * k, n * k)


def kernel(x, positions, e_norm, e_w_in, e_conv_w, e_conv_b, e_ga_w, e_ga_b, e_gx_w, e_gx_b, e_lambda,
           e_q_norm, e_w_uq, e_kv_norm, e_w_ukv, e_w_out, o_norm, o_w_in, o_idx_k_norm, o_w_out,
           m_norm, m_w1, m_w2, final_norm):
    b, s, d = x.shape
    m = b * s
    row = lambda v: v.reshape(1, -1).astype(F32)

    freqs = ROPE_THETA ** (-jnp.arange(0, MLA_ROPE, 2, dtype=F32) / MLA_ROPE)
    ang = positions.astype(F32)[..., None] * freqs
    cos2 = jnp.tile(jnp.cos(ang), (1, 1, 2)).reshape(m, MLA_ROPE)
    sin2 = jnp.tile(jnp.sin(ang), (1, 1, 2)).reshape(m, MLA_ROPE)
    z64 = jnp.zeros((m, 64), F32)
    qscale = LOG2E * (MLA_NOPE + MLA_ROPE) ** -0.5
    tq = qscale * jnp.concatenate([cos2, sin2, z64], axis=1)
    tc = jnp.concatenate([cos2, cos2, z64], axis=1)
    ts = jnp.concatenate([sin2, sin2, z64], axis=1)
    one_even = _one_hot_row(LANES, [MLA_ONE - LANES])
    vone_odd = _one_hot_row(DSA_KV_HEADS * LANES, [LANES * g + 64 for g in range(DSA_KV_HEADS)])
    low = (jnp.arange(TK)[None, :] < jnp.arange(TK)[:, None]).astype(BF16)
    w1, w2 = m_w1.astype(BF16), m_w2.astype(BF16)
    e_wo, o_wo = e_w_out.astype(BF16), o_w_out.astype(BF16)

    x2 = x.reshape(m, d)
    for l in range(DEPTH):
        j = l // 2
        if l % 2 == 0:
            w, wq, wuv = _even_weights(e_w_in[j], e_w_uq[j], e_w_ukv[j])
            xr, gr, q, kv = _even_in(x2, row(e_norm[j]), w, row(e_q_norm[j]), wq, row(e_kv_norm[j]),
                                     tq, tc, ts, one_even, b, s, qscale)
            cl = -LRU_C * jax.nn.softplus(-e_lambda[j].astype(F32))
            y_lru = _lru(xr.reshape(b, s, -1), gr.reshape(b, s, -1), e_conv_w[j].astype(F32),
                         row(e_conv_b[j]), _block_diag(e_ga_w[j]).astype(BF16), row(e_ga_b[j]),
                         _block_diag(e_gx_w[j]).astype(BF16), row(e_gx_b[j]), row(cl))
            y_mla = _mla(q, kv.reshape(b, s, -1), wuv)
            mix = [(y_lru.reshape(m, -1), e_wo, (j, 0)), (y_mla.reshape(m, -1), e_wo, (j, 1))]
        else:
            w, wwt = _odd_weights(o_w_in[j])
            kig = jnp.concatenate([row(o_idx_k_norm[j]), jnp.zeros((1, 128), F32), row(o_idx_k_norm[j])], axis=1)
            q, k, v, qi, ki, wit = _odd_in(x2, row(o_norm[j]), w, wwt, kig, vone_odd, b, s)
            mix = [(_dsa(q, k, v, qi, ki, wit, low).reshape(m, -1), o_wo, (j, 0))]
        x2 = _mix_mlp(x2, mix, row(m_norm[l]), w1, w2, l, row(final_norm), final=(l == DEPTH - 1))
    return x2.reshape(b, s, d)
```

```python
import functools
import math

import jax
import jax.numpy as jnp
from jax import lax
from jax.experimental import pallas as pl
from jax.experimental.pallas import tpu as pltpu

F32 = jnp.float32
BF16 = jnp.bfloat16
I32 = jnp.int32

D_MODEL = 1024
DEPTH = 4
CHUNK = 64
EPS = 1e-6

LRU_WIDTH = D_MODEL // 2
LRU_BLOCKS = 8
CONV_W = 4
LRU_C = 8.0

MLA_HEADS = 8
MLA_NOPE = 64
MLA_ROPE = 32
MLA_V = 64
MLA_Q_RANK = D_MODEL // 4
MLA_KV_RANK = D_MODEL // 8
ROPE_THETA = 10000.0

DSA_HEADS = 16
DSA_KV_HEADS = 4
DSA_HD = 64
IDX_HEADS = 8
IDX_HD = 64
TOPK_MAX = 256

D_FF = 4 * D_MODEL

LANES = 128
SUBLANES = 8
VMEM_LIMIT = 56 * 1024 * 1024
LOG2E = 1.4426950408889634
NEG = -1e30

TM = 1024
TQ = 256
TK = 256
LRU_T = 256
MLP_TM = 1024
MLP_TF = 1024
NSLAB = 8


def _params(sem):
    return pltpu.CompilerParams(dimension_semantics=sem, vmem_limit_bytes=VMEM_LIMIT)


def _rms(x, g):
    ms = jnp.mean(x * x, axis=-1, keepdims=True)
    return (x * lax.rsqrt(ms + EPS)) * g


def _dot(a, b):
    return jnp.dot(a, b, preferred_element_type=F32)


def _dot_nt(a, b):
    return lax.dot_general(a, b, (((1,), (1,)), ((), ())), preferred_element_type=F32)


def _slab_spec(nslab, rows, seq):
    nt = seq // rows
    return pl.BlockSpec((None, nslab, rows, LANES), lambda i: (i // nt, 0, i % nt, 0))


MLA_QW = 2 * LANES
MLA_ONE = LANES + 64


def _even_in_kernel(x_ref, g_ref, w_ref, qn_ref, wq_ref, kvn_ref, tq_ref, tc_ref, ts_ref, one_ref,
                    xr_ref, gr_ref, q_ref, kv_ref, *, qscale):
    h = _rms(x_ref[...], g_ref[...]).astype(BF16)
    xr_ref[...] = _dot(h, w_ref[:, 0:512])
    gr_ref[...] = _dot(h, w_ref[:, 512:1024])
    lat = _dot(h, w_ref[:, 1024:1664])
    krope = lat[:, 384:512] * tc_ref[...] + lat[:, 512:640] * ts_ref[...]
    kv_ref[:, 0:LANES] = _rms(lat[:, 256:384], kvn_ref[...]).astype(BF16)
    kv_ref[:, LANES:2 * LANES] = (krope + one_ref[...]).astype(BF16)
    cqn = _rms(lat[:, 0:256], qn_ref[...]).astype(BF16)
    tq = tq_ref[...]
    for c in range(MLA_HEADS // 2):
        qs = _dot(cqn, wq_ref[:, 2 * MLA_QW * c:2 * MLA_QW * (c + 1)])
        for e in range(2):
            o = MLA_QW * e
            q_ref[2 * c + e, :, 0:LANES] = (qs[:, o:o + LANES] * qscale).astype(BF16)
            q_ref[2 * c + e, :, LANES:MLA_QW] = (qs[:, o + LANES:o + MLA_QW] * tq).astype(BF16)


def _even_in(x2, g, w, qn, wq, kvn, tq, tc, ts, one, b, s, qscale):
    m = x2.shape[0]
    nt = s // TM
    row = lambda n: pl.BlockSpec((TM, n), lambda i: (i, 0))
    full = lambda a: pl.BlockSpec(a.shape, lambda i: (0, 0))
    sds = jax.ShapeDtypeStruct
    return pl.pallas_call(
        functools.partial(_even_in_kernel, qscale=qscale),
        grid=(m // TM,),
        in_specs=[row(D_MODEL), full(g), full(w), full(qn), full(wq), full(kvn),
                  row(LANES), row(LANES), row(LANES), full(one)],
        out_specs=[row(512), row(512),
                   pl.BlockSpec((None, MLA_HEADS, TM, MLA_QW), lambda i: (i // nt, 0, i % nt, 0)),
                   row(MLA_QW)],
        out_shape=[sds((m, 512), F32), sds((m, 512), F32), sds((b, MLA_HEADS, s, MLA_QW), BF16),
                   sds((m, MLA_QW), BF16)],
        compiler_params=_params(("parallel",)),
        name="even_in",
    )(x2, g, w, qn, wq, kvn, tq, tc, ts, one)


def _gelu_tanh(x):
    c = math.sqrt(2.0 / math.pi)
    return 0.5 * x * (1.0 + jnp.tanh(c * (x + 0.044715 * (x * x * x))))


def _lru_kernel(xr_ref, gr_ref, cw_ref, cb_ref, wa_ref, ba_ref, wx_ref, bx_ref, cl_ref, y_ref,
                xbuf, hc, a_scr, b_scr):
    t = LRU_T

    @pl.when(pl.program_id(1) == 0)
    def _():
        xbuf[0:8, :] = jnp.zeros((8, LRU_WIDTH), F32)
        hc[...] = jnp.zeros((1, LRU_WIDTH), F32)

    x = xr_ref[...]
    xbuf[8:8 + t, :] = x
    xc = cb_ref[...] + cw_ref[3:4, :] * x
    for j in range(CONV_W - 1):
        xc = xc + cw_ref[j:j + 1, :] * xbuf[5 + j:5 + j + t, :]
    xbuf[0:8, :] = x[t - 8:t, :]

    xcb = xc.astype(BF16)
    sigmoid = lambda z: 0.5 * jnp.tanh(0.5 * z) + 0.5
    r = sigmoid(_dot(xcb, wa_ref[...]) + ba_ref[...])
    i = sigmoid(_dot(xcb, wx_ref[...]) + bx_ref[...])
    log_a = cl_ref[...] * r
    a = jnp.exp(log_a)
    th = jnp.tanh(log_a)
    b = jnp.sqrt((-2.0 * th) / (1.0 - th)) * (i * xc)

    def doubling(a, b, axis):
        n = a.shape[axis]
        pos = lax.broadcasted_iota(I32, a.shape, axis)
        d = 1
        while d < n:
            keep = pos >= d
            b = jnp.where(keep, a * pltpu.roll(b, d, axis) + b, b)
            a = jnp.where(keep, a * pltpu.roll(a, d, axis), a)
            d *= 2
        return a, b

    ng = t // SUBLANES
    a, b = doubling(a.reshape(ng, SUBLANES, LRU_WIDTH), b.reshape(ng, SUBLANES, LRU_WIDTH), 1)
    a = a.reshape(t, LRU_WIDTH)
    b = b.reshape(t, LRU_WIDTH)
    nl = LRU_WIDTH // LANES
    for j in range(nl):
        a_scr[j] = a[:, LANES * j:LANES * (j + 1)]
        b_scr[j] = b[:, LANES * j:LANES * (j + 1)]
    ends = pl.ds(SUBLANES - 1, ng, stride=SUBLANES)
    grp = lax.broadcasted_iota(I32, (ng, LRU_WIDTH), 0)
    ae, be = doubling(jnp.concatenate([a_scr[j, ends, :] for j in range(nl)], axis=1),
                      jnp.concatenate([b_scr[j, ends, :] for j in range(nl)], axis=1), 0)
    h_end = ae * hc[...] + be
    h_in = jnp.where(grp >= 1, pltpu.roll(h_end, 1, 0), hc[...])
    hc[...] = h_end[ng - 1:ng, :]
    gate = _gelu_tanh(gr_ref[...])
    for g in range(0, ng, 2):
        hs = []
        for gg in (g, g + 1):
            rows = slice(SUBLANES * gg, SUBLANES * (gg + 1))
            hs.append(a[rows, :] * h_in[gg:gg + 1, :] + b[rows, :])
        rows = slice(SUBLANES * g, SUBLANES * (g + 2))
        y_ref[rows, :] = (jnp.concatenate(hs, axis=0) * gate[rows, :]).astype(BF16)


def _lru(xr, gr, cw, cb, wa, ba, wx, bx, cl):
    b, s, w = xr.shape
    blk = pl.BlockSpec((None, LRU_T, w), lambda i, j: (i, j, 0))
    full = lambda a: pl.BlockSpec(a.shape, lambda i, j: (0, 0))
    return pl.pallas_call(
        _lru_kernel,
        grid=(b, s // LRU_T),
        in_specs=[blk, blk, full(cw), full(cb), full(wa), full(ba), full(wx), full(bx), full(cl)],
        out_specs=blk,
        out_shape=jax.ShapeDtypeStruct((b, s, w), BF16),
        scratch_shapes=[pltpu.VMEM((LRU_T + 8, w), F32), pltpu.VMEM((1, w), F32),
                        pltpu.VMEM((w // LANES, LRU_T, LANES), F32),
                        pltpu.VMEM((w // LANES, LRU_T, LANES), F32)],
        compiler_params=_params(("parallel", "arbitrary")),
        name="rg_lru",
    )(xr, gr, cw, cb, wa, ba, wx, bx, cl)


def _key_rows(kb):
    return pl.ds(pl.multiple_of(kb * TK, TK), TK)


def _chunk_of(pos):
    return pos >> (CHUNK.bit_length() - 1)


def _admissible(c, kb):
    rc = _chunk_of(lax.broadcasted_iota(I32, (TQ, 1), 0) + c * TQ)
    cc = _chunk_of(lax.broadcasted_iota(I32, (1, TK), 1) + kb * TK)
    return cc <= rc


def _admissible_t(c, kb):
    kc = _chunk_of(lax.broadcasted_iota(I32, (TK, 1), 0) + kb * TK)
    qc = _chunk_of(lax.broadcasted_iota(I32, (1, TQ), 1) + c * TQ)
    return kc <= qc


def _flash_reset(m_scr, acc_scr):
    m_scr[...] = jnp.full(m_scr.shape, -jnp.inf, F32)
    acc_scr[...] = jnp.zeros(acc_scr.shape, F32)


def _flash_step(s, v, m_ref, acc_ref):
    m_old = m_ref[...]
    m_new = jnp.maximum(m_old, jnp.max(s, axis=1, keepdims=True))
    p = jnp.exp2(s - jnp.tile(m_new, (1, s.shape[1] // LANES))).astype(BF16)
    alpha = jnp.tile(jnp.exp2(m_old - m_new), (1, acc_ref.shape[1] // LANES))
    acc_ref[...] = alpha * acc_ref[...] + _dot(p, v)
    m_ref[...] = m_new


MLA_CHAINS = 2


def _mla_kernel(q_ref, kv_ref, wuv_ref, o_ref, m_scr, acc_scr):
    c = pl.program_id(1)
    kd = (c * TQ) // TK
    per = MLA_HEADS // MLA_CHAINS
    _flash_reset(m_scr, acc_scr)

    def tile(kb, bias):
        kv = kv_ref[_key_rows(kb), :]
        for ch in range(MLA_CHAINS):
            s = _dot_nt(q_ref[per * ch:per * (ch + 1)].reshape(per * TQ, MLA_QW), kv)
            if bias is not None:
                s = s + bias
            _flash_step(s, kv, m_scr.at[ch], acc_scr.at[ch])

    def past_tile(kb, carry):
        tile(kb, None)
        return carry

    lax.fori_loop(0, kd, past_tile, 0)
    dbias = jnp.where(_admissible(c, kd), 0.0, NEG).astype(F32)
    tile(kd, jnp.concatenate([dbias] * per, axis=0))

    def latent(hd):
        rows = slice((hd % per) * TQ, (hd % per + 1) * TQ)
        acc = acc_scr[hd // per, rows, :]
        return (acc[:, 0:LANES] * (1.0 / acc[:, MLA_ONE:MLA_ONE + 1])).astype(BF16)

    for p in range(MLA_HEADS // 2):
        y = _dot(latent(2 * p), wuv_ref[2 * p]) + _dot(latent(2 * p + 1), wuv_ref[2 * p + 1])
        o_ref[:, LANES * p:LANES * (p + 1)] = y.astype(BF16)


def _mla(q, kv, wuv):
    b, _, s, _ = q.shape
    assert TQ == TK
    per = MLA_HEADS // MLA_CHAINS
    return pl.pallas_call(
        _mla_kernel,
        grid=(b, s // TQ),
        in_specs=[pl.BlockSpec((None, MLA_HEADS, TQ, MLA_QW), lambda i, j: (i, 0, j, 0)),
                  pl.BlockSpec((None, s, MLA_QW), lambda i, j: (i, 0, 0)),
                  pl.BlockSpec(wuv.shape, lambda i, j: (0, 0, 0))],
        out_specs=pl.BlockSpec((None, TQ, 512), lambda i, j: (i, j, 0)),
        out_shape=jax.ShapeDtypeStruct((b, s, 512), BF16),
        scratch_shapes=[pltpu.VMEM((MLA_CHAINS, per * TQ, LANES), F32),
                        pltpu.VMEM((MLA_CHAINS, per * TQ, MLA_QW), F32)],
        compiler_params=_params(("parallel", "arbitrary")),
        name="mla_attn",
    )(q, kv, wuv)


def _odd_in_kernel(x_ref, g_ref, w_ref, wwt_ref, kig_ref, vone_ref,
                   q_ref, k_ref, v_ref, qi_ref, ki_ref, wit_ref):
    h = _rms(x_ref[...], g_ref[...]).astype(BF16)
    lane_slab = lambda a, j: a[:, LANES * j:LANES * (j + 1)]
    for c in range(2):
        qc = _dot(h, w_ref[:, 512 * c:512 * (c + 1)]) * LOG2E
        for j in range(4):
            q_ref[4 * c + j] = lane_slab(qc, j).astype(BF16)
    kc = _dot(h, w_ref[:, 1024:1536])
    vc = _dot(h, w_ref[:, 1536:2048]) + vone_ref[...]
    ic = _dot(h, w_ref[:, 2048:2560])
    for j in range(4):
        k_ref[j] = lane_slab(kc, j).astype(BF16)
        v_ref[j] = lane_slab(vc, j).astype(BF16)
        qi_ref[j] = lane_slab(ic, j).astype(BF16)
    kiab = _dot(h, w_ref[:, 2560:2816])
    kia = kiab[:, 0:LANES]
    kib = kiab[:, LANES:2 * LANES]
    inv = lax.rsqrt(jnp.sum(kia * kia, axis=-1, keepdims=True) * (1.0 / IDX_HD) + EPS)
    ki_ref[0] = ((kia * inv) * kig_ref[:, 0:LANES]).astype(BF16)
    ki_ref[1] = ((kib * inv) * kig_ref[:, LANES:2 * LANES]).astype(BF16)
    wit = _dot_nt(wwt_ref[...], h)[0:IDX_HEADS, :]
    wit_ref[...] = (wit * (IDX_HEADS ** -0.5)) * (IDX_HD ** -0.5)


def _odd_in(x2, g, w, wwt, kig, vone, b, s):
    m = x2.shape[0]
    row = lambda n: pl.BlockSpec((TM, n), lambda i: (i, 0))
    full = lambda a: pl.BlockSpec(a.shape, lambda i: (0, 0))
    sds = jax.ShapeDtypeStruct
    slab = lambda n: sds((b, n, s, LANES), BF16)
    return pl.pallas_call(
        _odd_in_kernel,
        grid=(m // TM,),
        in_specs=[row(D_MODEL), full(g), full(w), full(wwt), full(kig), full(vone)],
        out_specs=[_slab_spec(NSLAB, TM, s), _slab_spec(4, TM, s), _slab_spec(4, TM, s),
                   _slab_spec(4, TM, s), _slab_spec(2, TM, s),
                   pl.BlockSpec((IDX_HEADS, TM), lambda i: (0, i))],
        out_shape=[slab(NSLAB), slab(4), slab(4), slab(4), slab(2), sds((IDX_HEADS, m), F32)],
        compiler_params=_params(("parallel",)),
        name="odd_in",
    )(x2, g, w, wwt, kig, vone)


def _order_key(sc):
    bits = pltpu.bitcast(sc, I32)
    key = jnp.where(bits < 0, bits ^ jnp.int32(0x7FFFFFFF), bits)
    return jnp.where(sc == 0.0, 0, key)


def _dsa_kernel(q_ref, k_ref, v_ref, qi_ref, ki_ref, wit_ref, low_ref, o_ref,
                key_scr, bias_scr, q4_scr, m_scr, acc_scr, *, topk):
    c = pl.program_id(1)
    ntile = (c * TQ) // TK + 1
    part = 4 * SUBLANES

    def score_tile(kb, carry):
        rows = _key_rows(kb)
        score = jnp.zeros((TK, TQ), F32)
        for p in range(IDX_HEADS // 2):
            slab = qi_ref[p]
            for e in range(2):
                hd = 2 * p + e
                z = jnp.maximum(_dot_nt(ki_ref[e, rows, :], slab), 0.0)
                score = score + wit_ref[hd:hd + 1, :] * z
        score = jnp.where(_admissible_t(c, kb), score, -jnp.inf)
        key_scr[kb] = _order_key(score)
        return carry

    lax.fori_loop(0, ntile, score_tile, 0)

    def count(pred):
        def tile(kb, acc):
            hit = jnp.where(pred(key_scr[kb]), 1.0, 0.0)
            return acc + hit.reshape(TK // part, part, TQ).sum(axis=0)
        acc = lax.fori_loop(0, ntile, tile, jnp.zeros((part, TQ), F32))
        return jnp.sum(acc, axis=0, keepdims=True)

    def step(i, t):
        cand = t + jnp.left_shift(jnp.int32(1), 31 - i)
        return jnp.where(count(lambda k: k >= cand) >= float(topk), cand, t)

    nbits = jnp.where(ntile * TK <= topk, 0, 32)
    t = lax.fori_loop(0, nbits, step, jnp.full((1, TQ), -2 ** 31, I32))

    need = float(topk) - count(lambda k: k > t)

    def bias_tile(kb, ties_before):
        key = key_scr[kb]
        eq = jnp.where(key == t, 1.0, 0.0)
        prefix = _dot(low_ref[...], eq.astype(BF16)) + ties_before
        sel = jnp.where(key > t, 1.0, jnp.where(prefix < need, eq, 0.0))
        sel = jnp.where(_admissible_t(c, kb), sel, 0.0)
        bias_scr[kb] = jnp.where(sel > 0.0, 0.0, NEG).T
        return ties_before + jnp.sum(eq, axis=0, keepdims=True)

    lax.fori_loop(0, ntile, bias_tile, jnp.zeros((1, TQ), F32))

    rep = DSA_HEADS // DSA_KV_HEADS
    lane = lax.broadcasted_iota(I32, (TQ, LANES), 1)
    for gp in range(NSLAB):
        slab = q_ref[gp].astype(F32)
        q4_scr[gp // 2, pl.ds((2 * (gp % 2)) * TQ, TQ), :] = jnp.where(lane < 64, slab, 0.0).astype(BF16)
        q4_scr[gp // 2, pl.ds((2 * (gp % 2) + 1) * TQ, TQ), :] = jnp.where(
            lane < 64, pltpu.roll(slab, 64, 1), 0.0).astype(BF16)
    _flash_reset(m_scr, acc_scr)

    def attn_tile(kb, carry):
        rows = _key_rows(kb)
        bias = bias_scr[kb]
        bias4 = jnp.concatenate([bias] * rep, axis=0)
        for g in range(DSA_KV_HEADS):
            s = _dot_nt(q4_scr[g], k_ref[g, rows, :]) + bias4
            _flash_step(s, v_ref[g, rows, :], m_scr.at[g], acc_scr.at[g])
        return carry

    lax.fori_loop(0, ntile, attn_tile, 0)
    for gp in range(NSLAB):
        g, r0 = gp // 2, 2 * (gp % 2)
        acc_e = acc_scr[g, r0 * TQ:(r0 + 1) * TQ, :]
        acc_o = acc_scr[g, (r0 + 1) * TQ:(r0 + 2) * TQ, :]
        oe = acc_e * (1.0 / acc_e[:, 64:65])
        oo = pltpu.roll(acc_o * (1.0 / acc_o[:, 64:65]), 64, 1)
        o_ref[:, LANES * gp:LANES * (gp + 1)] = jnp.where(lane < 64, oe, oo).astype(BF16)


def _dsa(q, k, v, qi, ki, wit, low):
    b, _, s, _ = q.shape
    topk = min(TOPK_MAX, s // 4)
    assert topk <= TK and s % TK == 0 and TK % TQ == 0
    qblk = lambda n: pl.BlockSpec((None, n, TQ, LANES), lambda i, j: (i, 0, j, 0))
    kblk = lambda n: pl.BlockSpec((None, n, s, LANES), lambda i, j: (i, 0, 0, 0))
    nkb, nq = s // TK, s // TQ
    rep = DSA_HEADS // DSA_KV_HEADS
    return pl.pallas_call(
        functools.partial(_dsa_kernel, topk=topk),
        grid=(b, nq),
        in_specs=[qblk(NSLAB), kblk(DSA_KV_HEADS), kblk(DSA_KV_HEADS), qblk(4), kblk(2),
                  pl.BlockSpec((IDX_HEADS, TQ), lambda i, j: (0, i * nq + j)),
                  pl.BlockSpec((TK, TK), lambda i, j: (0, 0))],
        out_specs=pl.BlockSpec((None, TQ, 1024), lambda i, j: (i, j, 0)),
        out_shape=jax.ShapeDtypeStruct((b, s, 1024), BF16),
        scratch_shapes=[pltpu.VMEM((nkb, TK, TQ), I32), pltpu.VMEM((nkb, TQ, TK), F32),
                        pltpu.VMEM((DSA_KV_HEADS, rep * TQ, LANES), BF16),
                        pltpu.VMEM((DSA_KV_HEADS, rep * TQ, LANES), F32),
                        pltpu.VMEM((DSA_KV_HEADS, rep * TQ, LANES), F32)],
        compiler_params=_params(("parallel", "arbitrary")),
        name="dsa_attn",
    )(q, k, v, qi, ki, wit, low)


def _mix_mlp_kernel(*refs, n_mix, final):
    x_ref = refs[0]
    mix = refs[1:1 + 2 * n_mix]
    g_ref, w1_ref, w2_ref, fg_ref, o_ref, x1_scr, h_scr, acc = refs[1 + 2 * n_mix:]
    j = pl.program_id(1)

    @pl.when(j == 0)
    def _():
        x1 = x_ref[...]
        for a_ref, w_ref in zip(mix[0::2], mix[1::2]):
            x1 = x1 + _dot(a_ref[...], w_ref[...])
        x1_scr[...] = x1
        h_scr[...] = _rms(x1, g_ref[...]).astype(BF16)
        acc[...] = jnp.zeros_like(acc)

    a = jnp.maximum(_dot(h_scr[...], w1_ref[...]), 0.0)
    acc[...] += _dot((a * a).astype(BF16), w2_ref[...])

    @pl.when(j == pl.num_programs(1) - 1)
    def _():
        y = x1_scr[...] + acc[...]
        o_ref[...] = _rms(y, fg_ref[...]) if final else y


def _mix_mlp(x2, mix, g, w1, w2, layer, fg, final):
    m = x2.shape[0]
    specs, args = [pl.BlockSpec((MLP_TM, D_MODEL), lambda i, j: (i, 0))], [x2]
    for a, w, (n, rb) in mix:
        k = a.shape[1]
        specs += [pl.BlockSpec((MLP_TM, k), lambda i, j: (i, 0)),
                  pl.BlockSpec((None, k, D_MODEL), lambda i, j, n=n, rb=rb: (n, rb, 0))]
        args += [a, w]
    specs += [pl.BlockSpec((1, D_MODEL), lambda i, j: (0, 0)),
              pl.BlockSpec((None, D_MODEL, MLP_TF), lambda i, j: (layer, 0, j)),
              pl.BlockSpec((None, MLP_TF, D_MODEL), lambda i, j: (layer, j, 0)),
              pl.BlockSpec((1, D_MODEL), lambda i, j: (0, 0))]
    args += [g, w1, w2, fg]
    return pl.pallas_call(
        functools.partial(_mix_mlp_kernel, n_mix=len(mix), final=final),
        grid=(m // MLP_TM, D_FF // MLP_TF),
        in_specs=specs,
        out_specs=pl.BlockSpec((MLP_TM, D_MODEL), lambda i, j: (i, 0)),
        out_shape=jax.ShapeDtypeStruct((m, D_MODEL), F32),
        scratch_shapes=[pltpu.VMEM((MLP_TM, D_MODEL), F32), pltpu.VMEM((MLP_TM, D_MODEL), BF16),
                        pltpu.VMEM((MLP_TM, D_MODEL), F32)],
        compiler_params=_params(("parallel", "arbitrary")),
        name="mix_mlp",
    )(*args)


def _rot_cols(w):
    half = w.shape[-1] // 2
    return jnp.concatenate([-w[..., half:], w[..., :half]], axis=-1)


def _one_hot_row(n, idx):
    return jnp.zeros((1, n), F32).at[0, jnp.asarray(idx)].set(1.0)


def _even_weights(w_in, w_uq, w_ukv):
    d = w_in.shape[0]
    wkr = w_in[:, 1408:1440]
    z64 = jnp.zeros((d, 64), F32)
    w = jnp.concatenate([w_in[:, :1408], wkr, wkr, z64, _rot_cols(wkr), _rot_cols(wkr), z64], axis=1)
    uq = w_uq.reshape(MLA_Q_RANK, MLA_HEADS, MLA_NOPE + MLA_ROPE)
    ukv = w_ukv.reshape(MLA_KV_RANK, MLA_HEADS, MLA_NOPE + MLA_V)
    w_abs = jnp.einsum('rhd,lhd->rhl', uq[..., :MLA_NOPE], ukv[..., :MLA_NOPE],
                       precision=lax.Precision.HIGHEST)
    rope_w = uq[..., MLA_NOPE:]
    zq = jnp.zeros((MLA_Q_RANK, MLA_HEADS, MLA_QW - MLA_KV_RANK - 2 * MLA_ROPE), F32)
    wq = jnp.concatenate([w_abs, rope_w, _rot_cols(rope_w), zq], axis=-1)
    wq = wq.reshape(MLA_Q_RANK, MLA_HEADS * MLA_QW)
    vv = ukv[..., MLA_NOPE:].transpose(1, 0, 2)
    zv = jnp.zeros_like(vv)
    even = (jnp.arange(MLA_HEADS) % 2 == 0)[:, None, None]
    wuv = jnp.where(even, jnp.concatenate([vv, zv], axis=-1), jnp.concatenate([zv, vv], axis=-1))
    return w.astype(BF16), wq.astype(BF16), wuv.astype(BF16)


def _odd_weights(w_in):
    d = w_in.shape[0]
    wq = w_in[:, :1024] * (DSA_HD ** -0.5)
    wk = w_in[:, 1024:1280].reshape(d, DSA_KV_HEADS, DSA_HD)
    wv = w_in[:, 1280:1536].reshape(d, DSA_KV_HEADS, DSA_HD)
    z = jnp.zeros((d, DSA_KV_HEADS, DSA_HD), F32)
    pad = lambda a: jnp.concatenate([a, z], -1).reshape(d, -1)
    wqi = w_in[:, 1536:2048]
    wki = w_in[:, 2048:2112]
    z64 = jnp.zeros((d, 64), F32)
    w = jnp.concatenate([wq, pad(wk), pad(wv), wqi, wki, z64, z64, wki], axis=1)
    wwt = jnp.concatenate([w_in[:, 2112:2120].T, jnp.zeros((2 * SUBLANES - IDX_HEADS, d), F32)], axis=0)
    return w.astype(BF16), wwt.astype(BF16)


def _block_diag(w):
    n, k, _ = w.shape
    eye = jnp.eye(n, dtype=w.dtype)
    return (w[:, :, None, :] * eye[:, None, :, None]).reshape(n * k, n * k)


def kernel(x, positions, e_norm, e_w_in, e_conv_w, e_conv_b, e_ga_w, e_ga_b, e_gx_w, e_gx_b, e_lambda,
           e_q_norm, e_w_uq, e_kv_norm, e_w_ukv, e_w_out, o_norm, o_w_in, o_idx_k_norm, o_w_out,
           m_norm, m_w1, m_w2, final_norm):
    b, s, d = x.shape
    m = b * s
    row = lambda v: v.reshape(1, -1).astype(F32)

    freqs = ROPE_THETA ** (-jnp.arange(0, MLA_ROPE, 2, dtype=F32) / MLA_ROPE)
    ang = positions.astype(F32)[..., None] * freqs
    cos2 = jnp.tile(jnp.cos(ang), (1, 1, 2)).reshape(m, MLA_ROPE)
    sin2 = jnp.tile(jnp.sin(ang), (1, 1, 2)).reshape(m, MLA_ROPE)
    z64 = jnp.zeros((m, 64), F32)
    qscale = LOG2E * (MLA_NOPE + MLA_ROPE) ** -0.5
    tq = qscale * jnp.concatenate([cos2, sin2, z64], axis=1)
    tc = jnp.concatenate([cos2, cos2, z64], axis=1)
    ts = jnp.concatenate([sin2, sin2, z64], axis=1)
    one_even = _one_hot_row(LANES, [MLA_ONE - LANES])
    vone_odd = _one_hot_row(DSA_KV_HEADS * LANES, [LANES * g + 64 for g in range(DSA_KV_HEADS)])
    low = (jnp.arange(TK)[None, :] < jnp.arange(TK)[:, None]).astype(BF16)
    w1, w2 = m_w1.astype(BF16), m_w2.astype(BF16)
    e_wo, o_wo = e_w_out.astype(BF16), o_w_out.astype(BF16)

    x2 = x.reshape(m, d)
    for l in range(DEPTH):
        j = l // 2
        if l % 2 == 0:
            w, wq, wuv = _even_weights(e_w_in[j], e_w_uq[j], e_w_ukv[j])
            xr, gr, q, kv = _even_in(x2, row(e_norm[j]), w, row(e_q_norm[j]), wq, row(e_kv_norm[j]),
                                     tq, tc, ts, one_even, b, s, qscale)
            cl = -LRU_C * jax.nn.softplus(-e_lambda[j].astype(F32))
            y_lru = _lru(xr.reshape(b, s, -1), gr.reshape(b, s, -1), e_conv_w[j].astype(F32),
                         row(e_conv_b[j]), _block_diag(e_ga_w[j]).astype(BF16), row(e_ga_b[j]),
                         _block_diag(e_gx_w[j]).astype(BF16), row(e_gx_b[j]), row(cl))
            y_mla = _mla(q, kv.reshape(b, s, -1), wuv)
            mix = [(y_lru.reshape(m, -1), e_wo, (j, 0)), (y_mla.reshape(m, -1), e_wo, (j, 1))]
        else:
            w, wwt = _odd_weights(o_w_in[j])
            kig = jnp.concatenate([row(o_idx_k_norm[j]), jnp.zeros((1, 128), F32), row(o_idx_k_norm[j])], axis=1)
            q, k, v, qi, ki, wit = _odd_in(x2, row(o_norm[j]), w, wwt, kig, vone_odd, b, s)
            mix = [(_dsa(q, k, v, qi, ki, wit, low).reshape(m, -1), o_wo, (j, 0))]
        x2 = _mix_mlp(x2, mix, row(m_norm[l]), w1, w2, l, row(final_norm), final=(l == DEPTH - 1))
    return x2.reshape(b, s, d)
```

```python
import functools
import math

import jax
import jax.numpy as jnp
from jax import lax
from jax.experimental import pallas as pl
from jax.experimental.pallas import tpu as pltpu

F32 = jnp.float32
BF16 = jnp.bfloat16
I32 = jnp.int32

D_MODEL = 1024
DEPTH = 4
CHUNK = 64
EPS = 1e-6

LRU_WIDTH = D_MODEL // 2
LRU_BLOCKS = 8
CONV_W = 4
LRU_C = 8.0

MLA_HEADS = 8
MLA_NOPE = 64
MLA_ROPE = 32
MLA_V = 64
MLA_Q_RANK = D_MODEL // 4
MLA_KV_RANK = D_MODEL // 8
ROPE_THETA = 10000.0

DSA_HEADS = 16
DSA_KV_HEADS = 4
DSA_HD = 64
IDX_HEADS = 8
IDX_HD = 64
TOPK_MAX = 256

D_FF = 4 * D_MODEL

LANES = 128
SUBLANES = 8
VMEM_LIMIT = 56 * 1024 * 1024
LOG2E = 1.4426950408889634
NEG = -1e30

TM = 1024
TQ = 256
TK = 256
LRU_T = 256
MLP_TM = 1024
MLP_TF = 1024
NSLAB = 8


def _params(sem):
    return pltpu.CompilerParams(dimension_semantics=sem, vmem_limit_bytes=VMEM_LIMIT)


def _rms(x, g):
    ms = jnp.mean(x * x, axis=-1, keepdims=True)
    return (x * lax.rsqrt(ms + EPS)) * g


def _dot(a, b):
    return jnp.dot(a, b, preferred_element_type=F32)


def _dot_nt(a, b):
    return lax.dot_general(a, b, (((1,), (1,)), ((), ())), preferred_element_type=F32)


def _layout(*fields):
    cols, start = {}, 0
    for name, width in fields:
        cols[name] = (start, start + width)
        start += width
    return cols, start


def _slab_spec(nslab, rows, seq):
    nt = seq // rows
    return pl.BlockSpec((None, nslab, rows, LANES), lambda i: (i // nt, 0, i % nt, 0))


MLA_QW = 2 * LANES
MLA_ONE = LANES + 64
EVEN_COLS, EVEN_N = _layout(("xr", LRU_WIDTH), ("gr", LRU_WIDTH), ("cq", MLA_Q_RANK),
                            ("ckv", MLA_KV_RANK), ("kr", LANES), ("krs", LANES))


def _even_in_kernel(x_ref, g_ref, w_ref, qn_ref, wq_ref, kvn_ref, tq_ref, tc_ref, ts_ref, one_ref,
                    xr_ref, gr_ref, q_ref, kv_ref, *, qscale):
    h = _rms(x_ref[...], g_ref[...]).astype(BF16)
    cols = lambda name: slice(*EVEN_COLS[name])
    lat0 = EVEN_COLS["cq"][0]
    sub = lambda name: slice(EVEN_COLS[name][0] - lat0, EVEN_COLS[name][1] - lat0)
    xr_ref[...] = _dot(h, w_ref[:, cols("xr")])
    gr_ref[...] = _dot(h, w_ref[:, cols("gr")])
    lat = _dot(h, w_ref[:, lat0:EVEN_N])
    krope = lat[:, sub("kr")] * tc_ref[...] + lat[:, sub("krs")] * ts_ref[...]
    kv_ref[:, 0:LANES] = _rms(lat[:, sub("ckv")], kvn_ref[...]).astype(BF16)
    kv_ref[:, LANES:2 * LANES] = (krope + one_ref[...]).astype(BF16)
    cqn = _rms(lat[:, sub("cq")], qn_ref[...]).astype(BF16)
    tq = tq_ref[...]
    for c in range(MLA_HEADS // 2):
        qs = _dot(cqn, wq_ref[:, 2 * MLA_QW * c:2 * MLA_QW * (c + 1)])
        for e in range(2):
            o = MLA_QW * e
            q_ref[2 * c + e, :, 0:LANES] = (qs[:, o:o + LANES] * qscale).astype(BF16)
            q_ref[2 * c + e, :, LANES:MLA_QW] = (qs[:, o + LANES:o + MLA_QW] * tq).astype(BF16)


def _even_in(x2, g, w, qn, wq, kvn, tq, tc, ts, one, b, s, qscale):
    m = x2.shape[0]
    nt = s // TM
    row = lambda n: pl.BlockSpec((TM, n), lambda i: (i, 0))
    full = lambda a: pl.BlockSpec(a.shape, lambda i: (0, 0))
    sds = jax.ShapeDtypeStruct
    return pl.pallas_call(
        functools.partial(_even_in_kernel, qscale=qscale),
        grid=(m // TM,),
        in_specs=[row(D_MODEL), full(g), full(w), full(qn), full(wq), full(kvn),
                  row(LANES), row(LANES), row(LANES), full(one)],
        out_specs=[row(LRU_WIDTH), row(LRU_WIDTH),
                   pl.BlockSpec((None, MLA_HEADS, TM, MLA_QW), lambda i: (i // nt, 0, i % nt, 0)),
                   row(MLA_QW)],
        out_shape=[sds((m, LRU_WIDTH), F32), sds((m, LRU_WIDTH), F32),
                   sds((b, MLA_HEADS, s, MLA_QW), BF16),
                   sds((m, MLA_QW), BF16)],
        compiler_params=_params(("parallel",)),
        name="even_in",
    )(x2, g, w, qn, wq, kvn, tq, tc, ts, one)


def _gelu_tanh(x):
    c = math.sqrt(2.0 / math.pi)
    return 0.5 * x * (1.0 + jnp.tanh(c * (x + 0.044715 * (x * x * x))))


def _lru_kernel(xr_ref, gr_ref, cw_ref, cb_ref, wa_ref, ba_ref, wx_ref, bx_ref, cl_ref, y_ref,
                xbuf, hc, a_scr, b_scr):
    t = LRU_T

    @pl.when(pl.program_id(1) == 0)
    def _():
        xbuf[0:8, :] = jnp.zeros((8, LRU_WIDTH), F32)
        hc[...] = jnp.zeros((1, LRU_WIDTH), F32)

    x = xr_ref[...]
    xbuf[8:8 + t, :] = x
    xc = cb_ref[...] + cw_ref[3:4, :] * x
    for j in range(CONV_W - 1):
        xc = xc + cw_ref[j:j + 1, :] * xbuf[5 + j:5 + j + t, :]
    xbuf[0:8, :] = x[t - 8:t, :]

    xcb = xc.astype(BF16)
    sigmoid = lambda z: 0.5 * jnp.tanh(0.5 * z) + 0.5
    r = sigmoid(_dot(xcb, wa_ref[...]) + ba_ref[...])
    i = sigmoid(_dot(xcb, wx_ref[...]) + bx_ref[...])
    log_a = cl_ref[...] * r
    a = jnp.exp(log_a)
    th = jnp.tanh(log_a)
    b = jnp.sqrt((-2.0 * th) / (1.0 - th)) * (i * xc)

    def doubling(a, b, axis):
        n = a.shape[axis]
        pos = lax.broadcasted_iota(I32, a.shape, axis)
        d = 1
        while d < n:
            keep = pos >= d
            b = jnp.where(keep, a * pltpu.roll(b, d, axis) + b, b)
            a = jnp.where(keep, a * pltpu.roll(a, d, axis), a)
            d *= 2
        return a, b

    ng = t // SUBLANES
    a, b = doubling(a.reshape(ng, SUBLANES, LRU_WIDTH), b.reshape(ng, SUBLANES, LRU_WIDTH), 1)
    a = a.reshape(t, LRU_WIDTH)
    b = b.reshape(t, LRU_WIDTH)
    nl = LRU_WIDTH // LANES
    for j in range(nl):
        a_scr[j] = a[:, LANES * j:LANES * (j + 1)]
        b_scr[j] = b[:, LANES * j:LANES * (j + 1)]
    ends = pl.ds(SUBLANES - 1, ng, stride=SUBLANES)
    grp = lax.broadcasted_iota(I32, (ng, LRU_WIDTH), 0)
    ae, be = doubling(jnp.concatenate([a_scr[j, ends, :] for j in range(nl)], axis=1),
                      jnp.concatenate([b_scr[j, ends, :] for j in range(nl)], axis=1), 0)
    h_end = ae * hc[...] + be
    h_in = jnp.where(grp >= 1, pltpu.roll(h_end, 1, 0), hc[...])
    hc[...] = h_end[ng - 1:ng, :]
    gate = _gelu_tanh(gr_ref[...])
    for g in range(0, ng, 2):
        hs = []
        for gg in (g, g + 1):
            rows = slice(SUBLANES * gg, SUBLANES * (gg + 1))
            hs.append(a[rows, :] * h_in[gg:gg + 1, :] + b[rows, :])
        rows = slice(SUBLANES * g, SUBLANES * (g + 2))
        y_ref[rows, :] = (jnp.concatenate(hs, axis=0) * gate[rows, :]).astype(BF16)


def _lru(xr, gr, cw, cb, wa, ba, wx, bx, cl):
    b, s, w = xr.shape
    blk = pl.BlockSpec((None, LRU_T, w), lambda i, j: (i, j, 0))
    full = lambda a: pl.BlockSpec(a.shape, lambda i, j: (0, 0))
    return pl.pallas_call(
        _lru_kernel,
        grid=(b, s // LRU_T),
        in_specs=[blk, blk, full(cw), full(cb), full(wa), full(ba), full(wx), full(bx), full(cl)],
        out_specs=blk,
        out_shape=jax.ShapeDtypeStruct((b, s, w), BF16),
        scratch_shapes=[pltpu.VMEM((LRU_T + 8, w), F32), pltpu.VMEM((1, w), F32),
                        pltpu.VMEM((w // LANES, LRU_T, LANES), F32),
                        pltpu.VMEM((w // LANES, LRU_T, LANES), F32)],
        compiler_params=_params(("parallel", "arbitrary")),
        name="rg_lru",
    )(xr, gr, cw, cb, wa, ba, wx, bx, cl)


def _key_rows(kb):
    return pl.ds(pl.multiple_of(kb * TK, TK), TK)


def _chunk_of(pos):
    return pos >> (CHUNK.bit_length() - 1)


def _admissible(c, kb):
    rc = _chunk_of(lax.broadcasted_iota(I32, (TQ, 1), 0) + c * TQ)
    cc = _chunk_of(lax.broadcasted_iota(I32, (1, TK), 1) + kb * TK)
    return cc <= rc


def _admissible_t(c, kb):
    kc = _chunk_of(lax.broadcasted_iota(I32, (TK, 1), 0) + kb * TK)
    qc = _chunk_of(lax.broadcasted_iota(I32, (1, TQ), 1) + c * TQ)
    return kc <= qc


def _flash_reset(m_scr, acc_scr):
    m_scr[...] = jnp.full(m_scr.shape, -jnp.inf, F32)
    acc_scr[...] = jnp.zeros(acc_scr.shape, F32)


def _flash_step(s, v, m_ref, acc_ref):
    m_old = m_ref[...]
    m_new = jnp.maximum(m_old, jnp.max(s, axis=1, keepdims=True))
    p = jnp.exp2(s - jnp.tile(m_new, (1, s.shape[1] // LANES))).astype(BF16)
    alpha = jnp.tile(jnp.exp2(m_old - m_new), (1, acc_ref.shape[1] // LANES))
    acc_ref[...] = alpha * acc_ref[...] + _dot(p, v)
    m_ref[...] = m_new


MLA_CHAINS = 1


def _mla_kernel(q_ref, kv_ref, wuv_ref, o_ref, m_scr, acc_scr):
    c = pl.program_id(1)
    kd = (c * TQ) // TK
    per = MLA_HEADS // MLA_CHAINS
    _flash_reset(m_scr, acc_scr)

    def tile(kb, bias):
        kv = kv_ref[_key_rows(kb), :]
        for ch in range(MLA_CHAINS):
            s = _dot_nt(q_ref[per * ch:per * (ch + 1)].reshape(per * TQ, MLA_QW), kv)
            if bias is not None:
                s = s + bias
            _flash_step(s, kv, m_scr.at[ch], acc_scr.at[ch])

    def past_tile(kb, carry):
        tile(kb, None)
        return carry

    lax.fori_loop(0, kd, past_tile, 0)
    dbias = jnp.where(_admissible(c, kd), 0.0, NEG).astype(F32)
    tile(kd, jnp.concatenate([dbias] * per, axis=0))

    def latent(hd):
        rows = slice((hd % per) * TQ, (hd % per + 1) * TQ)
        acc = acc_scr[hd // per, rows, :]
        return (acc[:, 0:LANES] * (1.0 / acc[:, MLA_ONE:MLA_ONE + 1])).astype(BF16)

    for p in range(MLA_HEADS // 2):
        y = _dot(latent(2 * p), wuv_ref[2 * p]) + _dot(latent(2 * p + 1), wuv_ref[2 * p + 1])
        o_ref[:, LANES * p:LANES * (p + 1)] = y.astype(BF16)


def _mla(q, kv, wuv):
    b, _, s, _ = q.shape
    assert TQ == TK
    per = MLA_HEADS // MLA_CHAINS
    return pl.pallas_call(
        _mla_kernel,
        grid=(b, s // TQ),
        in_specs=[pl.BlockSpec((None, MLA_HEADS, TQ, MLA_QW), lambda i, j: (i, 0, j, 0)),
                  pl.BlockSpec((None, s, MLA_QW), lambda i, j: (i, 0, 0)),
                  pl.BlockSpec(wuv.shape, lambda i, j: (0, 0, 0))],
        out_specs=pl.BlockSpec((None, TQ, MLA_HEADS * MLA_V), lambda i, j: (i, j, 0)),
        out_shape=jax.ShapeDtypeStruct((b, s, MLA_HEADS * MLA_V), BF16),
        scratch_shapes=[pltpu.VMEM((MLA_CHAINS, per * TQ, LANES), F32),
                        pltpu.VMEM((MLA_CHAINS, per * TQ, MLA_QW), F32)],
        compiler_params=_params(("parallel", "arbitrary")),
        name="mla_attn",
    )(q, kv, wuv)


ODD_COLS, ODD_N = _layout(("q", DSA_HEADS * DSA_HD), ("k", DSA_KV_HEADS * LANES), ("v", DSA_KV_HEADS * LANES),
                          ("qi", IDX_HEADS * IDX_HD), ("ki", 2 * LANES))


def _odd_in_kernel(x_ref, g_ref, w_ref, wwt_ref, kig_ref, vone_ref,
                   q_ref, k_ref, v_ref, qi_ref, ki_ref, wit_ref):
    h = _rms(x_ref[...], g_ref[...]).astype(BF16)
    cols = lambda name: slice(*ODD_COLS[name])
    lane_slab = lambda a, j: a[:, LANES * j:LANES * (j + 1)]
    half = ODD_COLS["q"][1] // 2
    for c in range(2):
        qc = _dot(h, w_ref[:, half * c:half * (c + 1)]) * LOG2E
        for j in range(NSLAB // 2):
            q_ref[NSLAB // 2 * c + j] = lane_slab(qc, j).astype(BF16)
    kc = _dot(h, w_ref[:, cols("k")])
    vc = _dot(h, w_ref[:, cols("v")]) + vone_ref[...]
    ic = _dot(h, w_ref[:, cols("qi")])
    for j in range(DSA_KV_HEADS):
        k_ref[j] = lane_slab(kc, j).astype(BF16)
        v_ref[j] = lane_slab(vc, j).astype(BF16)
    for j in range(IDX_HEADS // 2):
        qi_ref[j] = lane_slab(ic, j).astype(BF16)
    kiab = _dot(h, w_ref[:, cols("ki")])
    kia = kiab[:, 0:LANES]
    kib = kiab[:, LANES:2 * LANES]
    inv = lax.rsqrt(jnp.sum(kia * kia, axis=-1, keepdims=True) * (1.0 / IDX_HD) + EPS)
    ki_ref[0] = ((kia * inv) * kig_ref[:, 0:LANES]).astype(BF16)
    ki_ref[1] = ((kib * inv) * kig_ref[:, LANES:2 * LANES]).astype(BF16)
    wit = _dot_nt(wwt_ref[...], h)[0:IDX_HEADS, :]
    wit_ref[...] = (wit * (IDX_HEADS ** -0.5)) * (IDX_HD ** -0.5)


def _odd_in(x2, g, w, wwt, kig, vone, b, s):
    m = x2.shape[0]
    row = lambda n: pl.BlockSpec((TM, n), lambda i: (i, 0))
    full = lambda a: pl.BlockSpec(a.shape, lambda i: (0, 0))
    sds = jax.ShapeDtypeStruct
    slab = lambda n: sds((b, n, s, LANES), BF16)
    return pl.pallas_call(
        _odd_in_kernel,
        grid=(m // TM,),
        in_specs=[row(D_MODEL), full(g), full(w), full(wwt), full(kig), full(vone)],
        out_specs=[_slab_spec(NSLAB, TM, s), _slab_spec(4, TM, s), _slab_spec(4, TM, s),
                   _slab_spec(4, TM, s), _slab_spec(2, TM, s),
                   pl.BlockSpec((IDX_HEADS, TM), lambda i: (0, i))],
        out_shape=[slab(NSLAB), slab(4), slab(4), slab(4), slab(2), sds((IDX_HEADS, m), F32)],
        compiler_params=_params(("parallel",)),
        name="odd_in",
    )(x2, g, w, wwt, kig, vone)


KEY_NEG_INF = -2 ** 31 + 0x7FFFFF


def _key_to_score(key):
    return pltpu.bitcast(jnp.where(key < 0, key ^ jnp.int32(0x7FFFFFFF), key), F32)


def _dsa_kernel(q_ref, k_ref, v_ref, qi_ref, ki_ref, wit_ref, low_ref, o_ref,
                key_scr, bias_scr, q4_scr, m_scr, acc_scr, *, topk):
    c = pl.program_id(1)
    ntile = (c * TQ) // TK + 1
    part = 4 * SUBLANES

    def score_tile(kb, carry):
        rows = _key_rows(kb)
        score = jnp.zeros((TK, TQ), F32)
        for p in range(IDX_HEADS // 2):
            slab = qi_ref[p]
            for e in range(2):
                hd = 2 * p + e
                z = jnp.maximum(_dot_nt(ki_ref[e, rows, :], slab), 0.0)
                score = score + wit_ref[hd:hd + 1, :] * z
        score = jnp.where(_admissible_t(c, kb), score, -jnp.inf)
        key_scr[kb] = jnp.where(score == 0.0, 0.0, score)
        return carry

    lax.fori_loop(0, ntile, score_tile, 0)

    def count(pred):
        def tile(kb, acc):
            hit = jnp.where(pred(key_scr[kb]), 1.0, 0.0)
            return acc + hit.reshape(TK // part, part, TQ).sum(axis=0)
        acc = lax.fori_loop(0, ntile, tile, jnp.zeros((part, TQ), F32))
        return jnp.sum(acc, axis=0, keepdims=True)

    total = (ntile * TK).astype(F32)

    def step(i, tkey):
        cand = tkey + jnp.left_shift(jnp.int32(1), 31 - i)
        cand_f = _key_to_score(cand)
        cnt = count(lambda sc: sc >= cand_f)
        cnt = jnp.where(cand <= KEY_NEG_INF, total, cnt)
        return jnp.where(cnt >= float(topk), cand, tkey)

    nbits = jnp.where(ntile * TK <= topk, 0, 32)
    tkey = lax.fori_loop(0, nbits, step, jnp.full((1, TQ), -2 ** 31, I32))
    t = _key_to_score(jnp.maximum(tkey, KEY_NEG_INF))

    need = float(topk) - count(lambda sc: sc > t)

    def bias_tile(kb, ties_before):
        key = key_scr[kb]
        eq = jnp.where(key == t, 1.0, 0.0)
        prefix = _dot(low_ref[...], eq.astype(BF16)) + ties_before
        sel = jnp.where(key > t, 1.0, jnp.where(prefix < need, eq, 0.0))
        sel = jnp.where(_admissible_t(c, kb), sel, 0.0)
        bias_scr[kb] = jnp.where(sel > 0.0, 0.0, NEG).T
        return ties_before + jnp.sum(eq, axis=0, keepdims=True)

    lax.fori_loop(0, ntile, bias_tile, jnp.zeros((1, TQ), F32))

    rep = DSA_HEADS // DSA_KV_HEADS
    lane = lax.broadcasted_iota(I32, (TQ, LANES), 1)
    low_half = lane < DSA_HD
    for gp in range(NSLAB):
        g, r0 = gp // 2, 2 * (gp % 2)
        slab = q_ref[gp].astype(F32)
        q4_scr[g, r0 * TQ:(r0 + 1) * TQ, :] = jnp.where(low_half, slab, 0.0).astype(BF16)
        q4_scr[g, (r0 + 1) * TQ:(r0 + 2) * TQ, :] = jnp.where(
            low_half, pltpu.roll(slab, DSA_HD, 1), 0.0).astype(BF16)
    _flash_reset(m_scr, acc_scr)

    def attn_tile(kb, carry):
        rows = _key_rows(kb)
        bias = bias_scr[kb]
        bias4 = jnp.concatenate([bias] * rep, axis=0)
        for g in range(DSA_KV_HEADS):
            s = _dot_nt(q4_scr[g], k_ref[g, rows, :]) + bias4
            _flash_step(s, v_ref[g, rows, :], m_scr.at[g], acc_scr.at[g])
        return carry

    lax.fori_loop(0, ntile, attn_tile, 0)
    for gp in range(NSLAB):
        g, r0 = gp // 2, 2 * (gp % 2)
        acc_e = acc_scr[g, r0 * TQ:(r0 + 1) * TQ, :]
        acc_o = acc_scr[g, (r0 + 1) * TQ:(r0 + 2) * TQ, :]
        oe = acc_e * (1.0 / acc_e[:, DSA_HD:DSA_HD + 1])
        oo = pltpu.roll(acc_o * (1.0 / acc_o[:, DSA_HD:DSA_HD + 1]), DSA_HD, 1)
        o_ref[:, LANES * gp:LANES * (gp + 1)] = jnp.where(low_half, oe, oo).astype(BF16)


def _dsa(q, k, v, qi, ki, wit, low):
    b, _, s, _ = q.shape
    topk = min(TOPK_MAX, s // 4)
    assert topk <= TK and s % TK == 0 and TK % TQ == 0
    qblk = lambda n: pl.BlockSpec((None, n, TQ, LANES), lambda i, j: (i, 0, j, 0))
    kblk = lambda n: pl.BlockSpec((None, n, s, LANES), lambda i, j: (i, 0, 0, 0))
    nkb, nq = s // TK, s // TQ
    rep = DSA_HEADS // DSA_KV_HEADS
    return pl.pallas_call(
        functools.partial(_dsa_kernel, topk=topk),
        grid=(b, nq),
        in_specs=[qblk(NSLAB), kblk(DSA_KV_HEADS), kblk(DSA_KV_HEADS), qblk(4), kblk(2),
                  pl.BlockSpec((IDX_HEADS, TQ), lambda i, j: (0, i * nq + j)),
                  pl.BlockSpec((TK, TK), lambda i, j: (0, 0))],
        out_specs=pl.BlockSpec((None, TQ, DSA_HEADS * DSA_HD), lambda i, j: (i, j, 0)),
        out_shape=jax.ShapeDtypeStruct((b, s, DSA_HEADS * DSA_HD), BF16),
        scratch_shapes=[pltpu.VMEM((nkb, TK, TQ), F32), pltpu.VMEM((nkb, TQ, TK), F32),
                        pltpu.VMEM((DSA_KV_HEADS, rep * TQ, LANES), BF16),
                        pltpu.VMEM((DSA_KV_HEADS, rep * TQ, LANES), F32),
                        pltpu.VMEM((DSA_KV_HEADS, rep * TQ, LANES), F32)],
        compiler_params=_params(("parallel", "arbitrary")),
        name="dsa_attn",
    )(q, k, v, qi, ki, wit, low)


def _mix_mlp_kernel(*refs, n_mix, final):
    x_ref = refs[0]
    mix = refs[1:1 + 2 * n_mix]
    g_ref, w1_ref, w2_ref, fg_ref, o_ref, x1_scr, h_scr, acc = refs[1 + 2 * n_mix:]
    j = pl.program_id(1)

    @pl.when(j == 0)
    def _():
        x1 = x_ref[...]
        for a_ref, w_ref in zip(mix[0::2], mix[1::2]):
            x1 = x1 + _dot(a_ref[...], w_ref[...])
        x1_scr[...] = x1
        h_scr[...] = _rms(x1, g_ref[...]).astype(BF16)
        acc[...] = jnp.zeros_like(acc)

    a = jnp.maximum(_dot(h_scr[...], w1_ref[...]), 0.0)
    acc[...] += _dot((a * a).astype(BF16), w2_ref[...])

    @pl.when(j == pl.num_programs(1) - 1)
    def _():
        y = x1_scr[...] + acc[...]
        o_ref[...] = _rms(y, fg_ref[...]) if final else y


def _mix_mlp(x2, mix, g, w1, w2, layer, fg, final):
    m = x2.shape[0]
    specs, args = [pl.BlockSpec((MLP_TM, D_MODEL), lambda i, j: (i, 0))], [x2]
    for a, w, (n, rb) in mix:
        k = a.shape[1]
        specs += [pl.BlockSpec((MLP_TM, k), lambda i, j: (i, 0)),
                  pl.BlockSpec((None, k, D_MODEL), lambda i, j, n=n, rb=rb: (n, rb, 0))]
        args += [a, w]
    specs += [pl.BlockSpec((1, D_MODEL), lambda i, j: (0, 0)),
              pl.BlockSpec((None, D_MODEL, MLP_TF), lambda i, j: (layer, 0, j)),
              pl.BlockSpec((None, MLP_TF, D_MODEL), lambda i, j: (layer, j, 0)),
              pl.BlockSpec((1, D_MODEL), lambda i, j: (0, 0))]
    args += [g, w1, w2, fg]
    return pl.pallas_call(
        functools.partial(_mix_mlp_kernel, n_mix=len(mix), final=final),
        grid=(m // MLP_TM, D_FF // MLP_TF),
        in_specs=specs,
        out_specs=pl.BlockSpec((MLP_TM, D_MODEL), lambda i, j: (i, 0)),
        out_shape=jax.ShapeDtypeStruct((m, D_MODEL), F32),
        scratch_shapes=[pltpu.VMEM((MLP_TM, D_MODEL), F32), pltpu.VMEM((MLP_TM, D_MODEL), BF16),
                        pltpu.VMEM((MLP_TM, D_MODEL), F32)],
        compiler_params=_params(("parallel", "arbitrary")),
        name="mix_mlp",
    )(*args)


def _rot_cols(w):
    half = w.shape[-1] // 2
    return jnp.concatenate([-w[..., half:], w[..., :half]], axis=-1)


def _one_hot_row(n, idx):
    return jnp.zeros((1, n), F32).at[0, jnp.asarray(idx)].set(1.0)


def _even_weights(w_in, w_uq, w_ukv):
    d = w_in.shape[0]
    kr0 = EVEN_COLS["ckv"][1]
    wkr = w_in[:, kr0:kr0 + MLA_ROPE]
    z64 = jnp.zeros((d, LANES - 2 * MLA_ROPE), F32)
    w = jnp.concatenate([w_in[:, :kr0], wkr, wkr, z64, _rot_cols(wkr), _rot_cols(wkr), z64], axis=1)
    uq = w_uq.reshape(MLA_Q_RANK, MLA_HEADS, MLA_NOPE + MLA_ROPE)
    ukv = w_ukv.reshape(MLA_KV_RANK, MLA_HEADS, MLA_NOPE + MLA_V)
    w_abs = jnp.einsum('rhd,lhd->rhl', uq[..., :MLA_NOPE], ukv[..., :MLA_NOPE],
                       precision=lax.Precision.HIGHEST)
    rope_w = uq[..., MLA_NOPE:]
    zq = jnp.zeros((MLA_Q_RANK, MLA_HEADS, MLA_QW - MLA_KV_RANK - 2 * MLA_ROPE), F32)
    wq = jnp.concatenate([w_abs, rope_w, _rot_cols(rope_w), zq], axis=-1)
    wq = wq.reshape(MLA_Q_RANK, MLA_HEADS * MLA_QW)
    vv = ukv[..., MLA_NOPE:].transpose(1, 0, 2)
    zv = jnp.zeros_like(vv)
    even = (jnp.arange(MLA_HEADS) % 2 == 0)[:, None, None]
    wuv = jnp.where(even, jnp.concatenate([vv, zv], axis=-1), jnp.concatenate([zv, vv], axis=-1))
    assert w.shape[1] == EVEN_N
    return w.astype(BF16), wq.astype(BF16), wuv.astype(BF16)


def _odd_weights(w_in):
    d = w_in.shape[0]
    src, _ = _layout(("q", DSA_HEADS * DSA_HD), ("k", DSA_KV_HEADS * DSA_HD), ("v", DSA_KV_HEADS * DSA_HD),
                     ("qi", IDX_HEADS * IDX_HD), ("ki", IDX_HD), ("wi", IDX_HEADS))
    part = lambda name: w_in[:, src[name][0]:src[name][1]]
    wq = part("q") * (DSA_HD ** -0.5)
    wk = part("k").reshape(d, DSA_KV_HEADS, DSA_HD)
    wv = part("v").reshape(d, DSA_KV_HEADS, DSA_HD)
    z = jnp.zeros((d, DSA_KV_HEADS, LANES - DSA_HD), F32)
    pad = lambda a: jnp.concatenate([a, z], -1).reshape(d, -1)
    wki = part("ki")
    z64 = jnp.zeros((d, LANES - IDX_HD), F32)
    w = jnp.concatenate([wq, pad(wk), pad(wv), part("qi"), wki, z64, z64, wki], axis=1)
    assert w.shape[1] == ODD_N
    wwt = jnp.concatenate([part("wi").T, jnp.zeros((2 * SUBLANES - IDX_HEADS, d), F32)], axis=0)
    return w.astype(BF16), wwt.astype(BF16)


def _block_diag(w):
    n, k, _ = w.shape
    eye = jnp.eye(n, dtype=w.dtype)
    return (w[:, :, None, :] * eye[:, None, :, None]).reshape(n * k, n * k)


def kernel(x, positions, e_norm, e_w_in, e_conv_w, e_conv_b, e_ga_w, e_ga_b, e_gx_w, e_gx_b, e_lambda,
           e_q_norm, e_w_uq, e_kv_norm, e_w_ukv, e_w_out, o_norm, o_w_in, o_idx_k_norm, o_w_out,
           m_norm, m_w1, m_w2, final_norm):
    b, s, d = x.shape
    m = b * s
    row = lambda v: v.reshape(1, -1).astype(F32)

    freqs = ROPE_THETA ** (-jnp.arange(0, MLA_ROPE, 2, dtype=F32) / MLA_ROPE)
    ang = positions.astype(F32)[..., None] * freqs
    cos2 = jnp.tile(jnp.cos(ang), (1, 1, 2)).reshape(m, MLA_ROPE)
    sin2 = jnp.tile(jnp.sin(ang), (1, 1, 2)).reshape(m, MLA_ROPE)
    z64 = jnp.zeros((m, LANES - 2 * MLA_ROPE), F32)
    qscale = LOG2E * (MLA_NOPE + MLA_ROPE) ** -0.5
    tq = qscale * jnp.concatenate([cos2, sin2, z64], axis=1)
    tc = jnp.concatenate([cos2, cos2, z64], axis=1)
    ts = jnp.concatenate([sin2, sin2, z64], axis=1)
    one_even = _one_hot_row(LANES, [MLA_ONE - LANES])
    vone_odd = _one_hot_row(DSA_KV_HEADS * LANES, [LANES * g + DSA_HD for g in range(DSA_KV_HEADS)])
    low = (jnp.arange(TK)[None, :] < jnp.arange(TK)[:, None]).astype(BF16)
    w1, w2 = m_w1.astype(BF16), m_w2.astype(BF16)
    e_wo, o_wo = e_w_out.astype(BF16), o_w_out.astype(BF16)

    x2 = x.reshape(m, d)
    for l in range(DEPTH):
        j = l // 2
        if l % 2 == 0:
            w, wq, wuv = _even_weights(e_w_in[j], e_w_uq[j], e_w_ukv[j])
            xr, gr, q, kv = _even_in(x2, row(e_norm[j]), w, row(e_q_norm[j]), wq, row(e_kv_norm[j]),
                                     tq, tc, ts, one_even, b, s, qscale)
            cl = -LRU_C * jax.nn.softplus(-e_lambda[j].astype(F32))
            y_lru = _lru(xr.reshape(b, s, -1), gr.reshape(b, s, -1), e_conv_w[j].astype(F32),
                         row(e_conv_b[j]), _block_diag(e_ga_w[j]).astype(BF16), row(e_ga_b[j]),
                         _block_diag(e_gx_w[j]).astype(BF16), row(e_gx_b[j]), row(cl))
            y_mla = _mla(q, kv.reshape(b, s, -1), wuv)
            mix = [(y_lru.reshape(m, -1), e_wo, (j, 0)), (y_mla.reshape(m, -1), e_wo, (j, 1))]
        else:
            w, wwt = _odd_weights(o_w_in[j])
            zpad = jnp.zeros((1, 2 * (LANES - IDX_HD)), F32)
            kig = jnp.concatenate([row(o_idx_k_norm[j]), zpad, row(o_idx_k_norm[j])], axis=1)
            q, k, v, qi, ki, wit = _odd_in(x2, row(o_norm[j]), w, wwt, kig, vone_odd, b, s)
            mix = [(_dsa(q, k, v, qi, ki, wit, low).reshape(m, -1), o_wo, (j, 0))]
        x2 = _mix_mlp(x2, mix, row(m_norm[l]), w1, w2, l, row(final_norm), final=(l == DEPTH - 1))
    return x2.reshape(b, s, d)
```

```python
import functools
import math

import jax
import jax.numpy as jnp
from jax import lax
from jax.experimental import pallas as pl
from jax.experimental.pallas import tpu as pltpu

F32 = jnp.float32
BF16 = jnp.bfloat16
I32 = jnp.int32

D_MODEL = 1024
DEPTH = 4
CHUNK = 64
EPS = 1e-6

LRU_WIDTH = D_MODEL // 2
LRU_BLOCKS = 8
CONV_W = 4
LRU_C = 8.0

MLA_HEADS = 8
MLA_NOPE = 64
MLA_ROPE = 32
MLA_V = 64
MLA_Q_RANK = D_MODEL // 4
MLA_KV_RANK = D_MODEL // 8
ROPE_THETA = 10000.0

DSA_HEADS = 16
DSA_KV_HEADS = 4
DSA_HD = 64
IDX_HEADS = 8
IDX_HD = 64
TOPK_MAX = 256

D_FF = 4 * D_MODEL

LANES = 128
SUBLANES = 8
VMEM_LIMIT = 56 * 1024 * 1024
LOG2E = 1.4426950408889634
NEG = -1e30

TM = 1024
TQ = 256
TK = 256
LRU_T = 256
MLP_TM = 1024
MLP_TF = 1024
NSLAB = 8


def _params(sem):
    return pltpu.CompilerParams(dimension_semantics=sem, vmem_limit_bytes=VMEM_LIMIT)


def _rms(x, g):
    ms = jnp.mean(x * x, axis=-1, keepdims=True)
    return (x * lax.rsqrt(ms + EPS)) * g


def _dot(a, b):
    return jnp.dot(a, b, preferred_element_type=F32)


def _dot_nt(a, b):
    return lax.dot_general(a, b, (((1,), (1,)), ((), ())), preferred_element_type=F32)


def _layout(*fields):
    cols, start = {}, 0
    for name, width in fields:
        cols[name] = (start, start + width)
        start += width
    return cols, start


def _slab_spec(nslab, rows, seq):
    nt = seq // rows
    return pl.BlockSpec((None, nslab, rows, LANES), lambda i: (i // nt, 0, i % nt, 0))


MLA_QW = 2 * LANES
MLA_ONE = LANES + 64
EVEN_COLS, EVEN_N = _layout(("xr", LRU_WIDTH), ("gr", LRU_WIDTH), ("cq", MLA_Q_RANK),
                            ("ckv", MLA_KV_RANK), ("kr", LANES), ("krs", LANES))


def _even_in_kernel(x_ref, g_ref, w_ref, qn_ref, wq_ref, kvn_ref, tq_ref, tc_ref, ts_ref, one_ref,
                    xr_ref, gr_ref, q_ref, kv_ref, *, qscale):
    h = _rms(x_ref[...], g_ref[...]).astype(BF16)
    cols = lambda name: slice(*EVEN_COLS[name])
    lat0 = EVEN_COLS["cq"][0]
    sub = lambda name: slice(EVEN_COLS[name][0] - lat0, EVEN_COLS[name][1] - lat0)
    xr_ref[...] = _dot(h, w_ref[:, cols("xr")])
    gr_ref[...] = _dot(h, w_ref[:, cols("gr")])
    lat = _dot(h, w_ref[:, lat0:EVEN_N])
    krope = lat[:, sub("kr")] * tc_ref[...] + lat[:, sub("krs")] * ts_ref[...]
    kv_ref[:, 0:LANES] = _rms(lat[:, sub("ckv")], kvn_ref[...]).astype(BF16)
    kv_ref[:, LANES:2 * LANES] = (krope + one_ref[...]).astype(BF16)
    cqn = _rms(lat[:, sub("cq")], qn_ref[...]).astype(BF16)
    tq = tq_ref[...]
    for c in range(MLA_HEADS // 2):
        qs = _dot(cqn, wq_ref[:, 2 * MLA_QW * c:2 * MLA_QW * (c + 1)])
        for e in range(2):
            o = MLA_QW * e
            q_ref[2 * c + e, :, 0:LANES] = (qs[:, o:o + LANES] * qscale).astype(BF16)
            q_ref[2 * c + e, :, LANES:MLA_QW] = (qs[:, o + LANES:o + MLA_QW] * tq).astype(BF16)


def _even_in(x2, g, w, qn, wq, kvn, tq, tc, ts, one, b, s, qscale):
    m = x2.shape[0]
    nt = s // TM
    row = lambda n: pl.BlockSpec((TM, n), lambda i: (i, 0))
    full = lambda a: pl.BlockSpec(a.shape, lambda i: (0, 0))
    sds = jax.ShapeDtypeStruct
    return pl.pallas_call(
        functools.partial(_even_in_kernel, qscale=qscale),
        grid=(m // TM,),
        in_specs=[row(D_MODEL), full(g), full(w), full(qn), full(wq), full(kvn),
                  row(LANES), row(LANES), row(LANES), full(one)],
        out_specs=[row(LRU_WIDTH), row(LRU_WIDTH),
                   pl.BlockSpec((None, MLA_HEADS, TM, MLA_QW), lambda i: (i // nt, 0, i % nt, 0)),
                   row(MLA_QW)],
        out_shape=[sds((m, LRU_WIDTH), F32), sds((m, LRU_WIDTH), F32),
                   sds((b, MLA_HEADS, s, MLA_QW), BF16),
                   sds((m, MLA_QW), BF16)],
        compiler_params=_params(("parallel",)),
        name="even_in",
    )(x2, g, w, qn, wq, kvn, tq, tc, ts, one)


def _gelu_tanh(x):
    c = math.sqrt(2.0 / math.pi)
    return 0.5 * x * (1.0 + jnp.tanh(c * (x + 0.044715 * (x * x * x))))


def _lru_kernel(xr_ref, gr_ref, cw_ref, cb_ref, wa_ref, ba_ref, wx_ref, bx_ref, cl_ref, y_ref,
                xbuf, hc, a_scr, b_scr):
    t = LRU_T

    @pl.when(pl.program_id(1) == 0)
    def _():
        xbuf[0:8, :] = jnp.zeros((8, LRU_WIDTH), F32)
        hc[...] = jnp.zeros((1, LRU_WIDTH), F32)

    x = xr_ref[...]
    xbuf[8:8 + t, :] = x
    xc = cb_ref[...] + cw_ref[3:4, :] * x
    for j in range(CONV_W - 1):
        xc = xc + cw_ref[j:j + 1, :] * xbuf[5 + j:5 + j + t, :]
    xbuf[0:8, :] = x[t - 8:t, :]

    xcb = xc.astype(BF16)
    sigmoid = lambda z: 0.5 * jnp.tanh(0.5 * z) + 0.5
    r = sigmoid(_dot(xcb, wa_ref[...]) + ba_ref[...])
    i = sigmoid(_dot(xcb, wx_ref[...]) + bx_ref[...])
    log_a = cl_ref[...] * r
    a = jnp.exp(log_a)
    th = jnp.tanh(log_a)
    b = jnp.sqrt((-2.0 * th) / (1.0 - th)) * (i * xc)

    def doubling(a, b, axis):
        n = a.shape[axis]
        pos = lax.broadcasted_iota(I32, a.shape, axis)
        d = 1
        while d < n:
            keep = pos >= d
            b = jnp.where(keep, a * pltpu.roll(b, d, axis) + b, b)
            a = jnp.where(keep, a * pltpu.roll(a, d, axis), a)
            d *= 2
        return a, b

    ng = t // SUBLANES
    a, b = doubling(a.reshape(ng, SUBLANES, LRU_WIDTH), b.reshape(ng, SUBLANES, LRU_WIDTH), 1)
    a = a.reshape(t, LRU_WIDTH)
    b = b.reshape(t, LRU_WIDTH)
    nl = LRU_WIDTH // LANES
    for j in range(nl):
        a_scr[j] = a[:, LANES * j:LANES * (j + 1)]
        b_scr[j] = b[:, LANES * j:LANES * (j + 1)]
    ends = pl.ds(SUBLANES - 1, ng, stride=SUBLANES)
    grp = lax.broadcasted_iota(I32, (ng, LRU_WIDTH), 0)
    ae, be = doubling(jnp.concatenate([a_scr[j, ends, :] for j in range(nl)], axis=1),
                      jnp.concatenate([b_scr[j, ends, :] for j in range(nl)], axis=1), 0)
    h_end = ae * hc[...] + be
    h_in = jnp.where(grp >= 1, pltpu.roll(h_end, 1, 0), hc[...])
    hc[...] = h_end[ng - 1:ng, :]
    gate = _gelu_tanh(gr_ref[...])
    for g in range(0, ng, 2):
        hs = []
        for gg in (g, g + 1):
            rows = slice(SUBLANES * gg, SUBLANES * (gg + 1))
            hs.append(a[rows, :] * h_in[gg:gg + 1, :] + b[rows, :])
        rows = slice(SUBLANES * g, SUBLANES * (g + 2))
        y_ref[rows, :] = (jnp.concatenate(hs, axis=0) * gate[rows, :]).astype(BF16)


def _lru(xr, gr, cw, cb, wa, ba, wx, bx, cl):
    b, s, w = xr.shape
    blk = pl.BlockSpec((None, LRU_T, w), lambda i, j: (i, j, 0))
    full = lambda a: pl.BlockSpec(a.shape, lambda i, j: (0, 0))
    return pl.pallas_call(
        _lru_kernel,
        grid=(b, s // LRU_T),
        in_specs=[blk, blk, full(cw), full(cb), full(wa), full(ba), full(wx), full(bx), full(cl)],
        out_specs=blk,
        out_shape=jax.ShapeDtypeStruct((b, s, w), BF16),
        scratch_shapes=[pltpu.VMEM((LRU_T + 8, w), F32), pltpu.VMEM((1, w), F32),
                        pltpu.VMEM((w // LANES, LRU_T, LANES), F32),
                        pltpu.VMEM((w // LANES, LRU_T, LANES), F32)],
        compiler_params=_params(("parallel", "arbitrary")),
        name="rg_lru",
    )(xr, gr, cw, cb, wa, ba, wx, bx, cl)


def _for_tiles(n, body, init):
    def pair(kp, carry):
        return body(2 * kp + 1, body(2 * kp, carry))

    carry = lax.fori_loop(0, n // 2, pair, init)
    return lax.cond(n % 2 == 1, lambda cr: body(n - 1, cr), lambda cr: cr, carry)


def _key_rows(kb):
    return pl.ds(pl.multiple_of(kb * TK, TK), TK)


def _chunk_of(pos):
    return pos >> (CHUNK.bit_length() - 1)


def _admissible(c, kb):
    rc = _chunk_of(lax.broadcasted_iota(I32, (TQ, 1), 0) + c * TQ)
    cc = _chunk_of(lax.broadcasted_iota(I32, (1, TK), 1) + kb * TK)
    return cc <= rc


def _admissible_t(c, kb):
    kc = _chunk_of(lax.broadcasted_iota(I32, (TK, 1), 0) + kb * TK)
    qc = _chunk_of(lax.broadcasted_iota(I32, (1, TQ), 1) + c * TQ)
    return kc <= qc


def _flash_reset(m_scr, acc_scr):
    m_scr[...] = jnp.full(m_scr.shape, -jnp.inf, F32)
    acc_scr[...] = jnp.zeros(acc_scr.shape, F32)


def _flash_step(s, v, m_ref, acc_ref):
    m_old = m_ref[...]
    m_new = jnp.maximum(m_old, jnp.max(s, axis=1, keepdims=True))
    p = jnp.exp2(s - jnp.tile(m_new, (1, s.shape[1] // LANES))).astype(BF16)
    alpha = jnp.tile(jnp.exp2(m_old - m_new), (1, acc_ref.shape[1] // LANES))
    acc_ref[...] = alpha * acc_ref[...] + _dot(p, v)
    m_ref[...] = m_new


MLA_CHAINS = 1


def _mla_kernel(q_ref, kv_ref, wuv_ref, o_ref, m_scr, acc_scr):
    c = pl.program_id(1)
    kd = (c * TQ) // TK
    per = MLA_HEADS // MLA_CHAINS
    _flash_reset(m_scr, acc_scr)

    def tile(kb, bias):
        kv = kv_ref[_key_rows(kb), :]
        for ch in range(MLA_CHAINS):
            s = _dot_nt(q_ref[per * ch:per * (ch + 1)].reshape(per * TQ, MLA_QW), kv)
            if bias is not None:
                s = s + bias
            _flash_step(s, kv, m_scr.at[ch], acc_scr.at[ch])

    def past_tile(kb, carry):
        tile(kb, None)
        return carry

    _for_tiles(kd, past_tile, 0)
    dbias = jnp.where(_admissible(c, kd), 0.0, NEG).astype(F32)
    tile(kd, jnp.concatenate([dbias] * per, axis=0))

    def latent(hd):
        rows = slice((hd % per) * TQ, (hd % per + 1) * TQ)
        acc = acc_scr[hd // per, rows, :]
        return (acc[:, 0:LANES] * (1.0 / acc[:, MLA_ONE:MLA_ONE + 1])).astype(BF16)

    for p in range(MLA_HEADS // 2):
        y = _dot(latent(2 * p), wuv_ref[2 * p]) + _dot(latent(2 * p + 1), wuv_ref[2 * p + 1])
        o_ref[:, LANES * p:LANES * (p + 1)] = y.astype(BF16)


def _mla(q, kv, wuv):
    b, _, s, _ = q.shape
    assert TQ == TK
    per = MLA_HEADS // MLA_CHAINS
    return pl.pallas_call(
        _mla_kernel,
        grid=(b, s // TQ),
        in_specs=[pl.BlockSpec((None, MLA_HEADS, TQ, MLA_QW), lambda i, j: (i, 0, j, 0)),
                  pl.BlockSpec((None, s, MLA_QW), lambda i, j: (i, 0, 0)),
                  pl.BlockSpec(wuv.shape, lambda i, j: (0, 0, 0))],
        out_specs=pl.BlockSpec((None, TQ, MLA_HEADS * MLA_V), lambda i, j: (i, j, 0)),
        out_shape=jax.ShapeDtypeStruct((b, s, MLA_HEADS * MLA_V), BF16),
        scratch_shapes=[pltpu.VMEM((MLA_CHAINS, per * TQ, LANES), F32),
                        pltpu.VMEM((MLA_CHAINS, per * TQ, MLA_QW), F32)],
        compiler_params=_params(("parallel", "arbitrary")),
        name="mla_attn",
    )(q, kv, wuv)


ODD_COLS, ODD_N = _layout(("q", DSA_HEADS * DSA_HD), ("k", DSA_KV_HEADS * LANES), ("v", DSA_KV_HEADS * LANES),
                          ("qi", IDX_HEADS * IDX_HD), ("ki", 2 * LANES))


def _odd_in_kernel(x_ref, g_ref, w_ref, wwt_ref, kig_ref, vone_ref,
                   q_ref, k_ref, v_ref, qi_ref, ki_ref, wit_ref):
    h = _rms(x_ref[...], g_ref[...]).astype(BF16)
    cols = lambda name: slice(*ODD_COLS[name])
    lane_slab = lambda a, j: a[:, LANES * j:LANES * (j + 1)]
    half = ODD_COLS["q"][1] // 2
    for c in range(2):
        qc = _dot(h, w_ref[:, half * c:half * (c + 1)]) * LOG2E
        for j in range(NSLAB // 2):
            q_ref[NSLAB // 2 * c + j] = lane_slab(qc, j).astype(BF16)
    kc = _dot(h, w_ref[:, cols("k")])
    vc = _dot(h, w_ref[:, cols("v")]) + vone_ref[...]
    ic = _dot(h, w_ref[:, cols("qi")])
    for j in range(DSA_KV_HEADS):
        k_ref[j] = lane_slab(kc, j).astype(BF16)
        v_ref[j] = lane_slab(vc, j).astype(BF16)
    for j in range(IDX_HEADS // 2):
        qi_ref[j] = lane_slab(ic, j).astype(BF16)
    kiab = _dot(h, w_ref[:, cols("ki")])
    kia = kiab[:, 0:LANES]
    kib = kiab[:, LANES:2 * LANES]
    inv = lax.rsqrt(jnp.sum(kia * kia, axis=-1, keepdims=True) * (1.0 / IDX_HD) + EPS)
    ki_ref[0] = ((kia * inv) * kig_ref[:, 0:LANES]).astype(BF16)
    ki_ref[1] = ((kib * inv) * kig_ref[:, LANES:2 * LANES]).astype(BF16)
    wit = _dot_nt(wwt_ref[...], h)[0:IDX_HEADS, :]
    wit_ref[...] = (wit * (IDX_HEADS ** -0.5)) * (IDX_HD ** -0.5)


def _odd_in(x2, g, w, wwt, kig, vone, b, s):
    m = x2.shape[0]
    row = lambda n: pl.BlockSpec((TM, n), lambda i: (i, 0))
    full = lambda a: pl.BlockSpec(a.shape, lambda i: (0, 0))
    sds = jax.ShapeDtypeStruct
    slab = lambda n: sds((b, n, s, LANES), BF16)
    return pl.pallas_call(
        _odd_in_kernel,
        grid=(m // TM,),
        in_specs=[row(D_MODEL), full(g), full(w), full(wwt), full(kig), full(vone)],
        out_specs=[_slab_spec(NSLAB, TM, s), _slab_spec(4, TM, s), _slab_spec(4, TM, s),
                   _slab_spec(4, TM, s), _slab_spec(2, TM, s),
                   pl.BlockSpec((IDX_HEADS, TM), lambda i: (0, i))],
        out_shape=[slab(NSLAB), slab(4), slab(4), slab(4), slab(2), sds((IDX_HEADS, m), F32)],
        compiler_params=_params(("parallel",)),
        name="odd_in",
    )(x2, g, w, wwt, kig, vone)


KEY_NEG_INF = -2 ** 31 + 0x7FFFFF


def _key_to_score(key):
    return pltpu.bitcast(jnp.where(key < 0, key ^ jnp.int32(0x7FFFFFFF), key), F32)


def _dsa_kernel(q_ref, k_ref, v_ref, qi_ref, ki_ref, wit_ref, low_ref, o_ref,
                key_scr, bias_scr, q4_scr, m_scr, acc_scr, *, topk):
    c = pl.program_id(1)
    ntile = (c * TQ) // TK + 1
    part = 4 * SUBLANES

    def score_tile(kb, carry):
        rows = _key_rows(kb)
        score = jnp.zeros((TK, TQ), F32)
        for p in range(IDX_HEADS // 2):
            slab = qi_ref[p]
            for e in range(2):
                hd = 2 * p + e
                z = jnp.maximum(_dot_nt(ki_ref[e, rows, :], slab), 0.0)
                score = score + wit_ref[hd:hd + 1, :] * z
        score = jnp.where(_admissible_t(c, kb), score, -jnp.inf)
        key_scr[kb] = jnp.where(score == 0.0, 0.0, score)
        return carry

    _for_tiles(ntile, score_tile, 0)

    def count(pred):
        def tile(kb, acc):
            hit = jnp.where(pred(key_scr[kb]), 1.0, 0.0)
            return acc + hit.reshape(TK // part, part, TQ).sum(axis=0)
        acc = _for_tiles(ntile, tile, jnp.zeros((part, TQ), F32))
        return jnp.sum(acc, axis=0, keepdims=True)

    total = (ntile * TK).astype(F32)

    def step(i, tkey):
        cand = tkey + jnp.left_shift(jnp.int32(1), 31 - i)
        cand_f = _key_to_score(cand)
        cnt = count(lambda sc: sc >= cand_f)
        cnt = jnp.where(cand <= KEY_NEG_INF, total, cnt)
        return jnp.where(cnt >= float(topk), cand, tkey)

    nbits = jnp.where(ntile * TK <= topk, 0, 32)
    tkey = lax.fori_loop(0, nbits, step, jnp.full((1, TQ), -2 ** 31, I32))
    t = _key_to_score(jnp.maximum(tkey, KEY_NEG_INF))

    need = float(topk) - count(lambda sc: sc > t)

    def bias_tile(kb, ties_before):
        key = key_scr[kb]
        eq = jnp.where(key == t, 1.0, 0.0)
        prefix = _dot(low_ref[...], eq.astype(BF16)) + ties_before
        sel = jnp.where(key > t, 1.0, jnp.where(prefix < need, eq, 0.0))
        sel = jnp.where(_admissible_t(c, kb), sel, 0.0)
        bias_scr[kb] = jnp.where(sel > 0.0, 0.0, NEG).T
        return ties_before + jnp.sum(eq, axis=0, keepdims=True)

    _for_tiles(ntile, bias_tile, jnp.zeros((1, TQ), F32))

    rep = DSA_HEADS // DSA_KV_HEADS
    lane = lax.broadcasted_iota(I32, (TQ, LANES), 1)
    low_half = lane < DSA_HD
    for gp in range(NSLAB):
        g, r0 = gp // 2, 2 * (gp % 2)
        slab = q_ref[gp].astype(F32)
        q4_scr[g, r0 * TQ:(r0 + 1) * TQ, :] = jnp.where(low_half, slab, 0.0).astype(BF16)
        q4_scr[g, (r0 + 1) * TQ:(r0 + 2) * TQ, :] = jnp.where(
            low_half, pltpu.roll(slab, DSA_HD, 1), 0.0).astype(BF16)
    _flash_reset(m_scr, acc_scr)

    def attn_tile(kb, carry):
        rows = _key_rows(kb)
        bias = bias_scr[kb]
        bias4 = jnp.concatenate([bias] * rep, axis=0)
        for g in range(DSA_KV_HEADS):
            s = _dot_nt(q4_scr[g], k_ref[g, rows, :]) + bias4
            _flash_step(s, v_ref[g, rows, :], m_scr.at[g], acc_scr.at[g])
        return carry

    _for_tiles(ntile, attn_tile, 0)

    for gp in range(NSLAB):
        g, r0 = gp // 2, 2 * (gp % 2)
        acc_e = acc_scr[g, r0 * TQ:(r0 + 1) * TQ, :]
        acc_o = acc_scr[g, (r0 + 1) * TQ:(r0 + 2) * TQ, :]
        oe = acc_e * (1.0 / acc_e[:, DSA_HD:DSA_HD + 1])
        oo = pltpu.roll(acc_o * (1.0 / acc_o[:, DSA_HD:DSA_HD + 1]), DSA_HD, 1)
        o_ref[:, LANES * gp:LANES * (gp + 1)] = jnp.where(low_half, oe, oo).astype(BF16)


def _dsa(q, k, v, qi, ki, wit, low):
    b, _, s, _ = q.shape
    topk = min(TOPK_MAX, s // 4)
    assert topk <= TK and s % TK == 0 and TK % TQ == 0
    qblk = lambda n: pl.BlockSpec((None, n, TQ, LANES), lambda i, j: (i, 0, j, 0))
    kblk = lambda n: pl.BlockSpec((None, n, s, LANES), lambda i, j: (i, 0, 0, 0))
    nkb, nq = s // TK, s // TQ
    rep = DSA_HEADS // DSA_KV_HEADS
    return pl.pallas_call(
        functools.partial(_dsa_kernel, topk=topk),
        grid=(b, nq),
        in_specs=[qblk(NSLAB), kblk(DSA_KV_HEADS), kblk(DSA_KV_HEADS), qblk(4), kblk(2),
                  pl.BlockSpec((IDX_HEADS, TQ), lambda i, j: (0, i * nq + j)),
                  pl.BlockSpec((TK, TK), lambda i, j: (0, 0))],
        out_specs=pl.BlockSpec((None, TQ, DSA_HEADS * DSA_HD), lambda i, j: (i, j, 0)),
        out_shape=jax.ShapeDtypeStruct((b, s, DSA_HEADS * DSA_HD), BF16),
        scratch_shapes=[pltpu.VMEM((nkb, TK, TQ), F32), pltpu.VMEM((nkb, TQ, TK), F32),
                        pltpu.VMEM((DSA_KV_HEADS, rep * TQ, LANES), BF16),
                        pltpu.VMEM((DSA_KV_HEADS, rep * TQ, LANES), F32),
                        pltpu.VMEM((DSA_KV_HEADS, rep * TQ, LANES), F32)],
        compiler_params=_params(("parallel", "arbitrary")),
        name="dsa_attn",
    )(q, k, v, qi, ki, wit, low)


def _mix_mlp_kernel(*refs, n_mix, final):
    x_ref = refs[0]
    mix = refs[1:1 + 2 * n_mix]
    g_ref, w1_ref, w2_ref, fg_ref, o_ref, x1_scr, h_scr, acc = refs[1 + 2 * n_mix:]
    j = pl.program_id(1)

    @pl.when(j == 0)
    def _():
        x1 = x_ref[...]
        for a_ref, w_ref in zip(mix[0::2], mix[1::2]):
            x1 = x1 + _dot(a_ref[...], w_ref[...])
        x1_scr[...] = x1
        h_scr[...] = _rms(x1, g_ref[...]).astype(BF16)
        acc[...] = jnp.zeros_like(acc)

    a = jnp.maximum(_dot(h_scr[...], w1_ref[...]), 0.0)
    acc[...] += _dot((a * a).astype(BF16), w2_ref[...])

    @pl.when(j == pl.num_programs(1) - 1)
    def _():
        y = x1_scr[...] + acc[...]
        o_ref[...] = _rms(y, fg_ref[...]) if final else y


def _mix_mlp(x2, mix, g, w1, w2, layer, fg, final):
    m = x2.shape[0]
    specs, args = [pl.BlockSpec((MLP_TM, D_MODEL), lambda i, j: (i, 0))], [x2]
    for a, w, (n, rb) in mix:
        k = a.shape[1]
        specs += [pl.BlockSpec((MLP_TM, k), lambda i, j: (i, 0)),
                  pl.BlockSpec((None, k, D_MODEL), lambda i, j, n=n, rb=rb: (n, rb, 0))]
        args += [a, w]
    specs += [pl.BlockSpec((1, D_MODEL), lambda i, j: (0, 0)),
              pl.BlockSpec((None, D_MODEL, MLP_TF), lambda i, j: (layer, 0, j)),
              pl.BlockSpec((None, MLP_TF, D_MODEL), lambda i, j: (layer, j, 0)),
              pl.BlockSpec((1, D_MODEL), lambda i, j: (0, 0))]
    args += [g, w1, w2, fg]
    return pl.pallas_call(
        functools.partial(_mix_mlp_kernel, n_mix=len(mix), final=final),
        grid=(m // MLP_TM, D_FF // MLP_TF),
        in_specs=specs,
        out_specs=pl.BlockSpec((MLP_TM, D_MODEL), lambda i, j: (i, 0)),
        out_shape=jax.ShapeDtypeStruct((m, D_MODEL), F32),
        scratch_shapes=[pltpu.VMEM((MLP_TM, D_MODEL), F32), pltpu.VMEM((MLP_TM, D_MODEL), BF16),
                        pltpu.VMEM((MLP_TM, D_MODEL), F32)],
        compiler_params=_params(("parallel", "arbitrary")),
        name="mix_mlp",
    )(*args)


def _rot_cols(w):
    half = w.shape[-1] // 2
    return jnp.concatenate([-w[..., half:], w[..., :half]], axis=-1)


def _one_hot_row(n, idx):
    return jnp.zeros((1, n), F32).at[0, jnp.asarray(idx)].set(1.0)


def _even_weights(w_in, w_uq, w_ukv):
    d = w_in.shape[0]
    kr0 = EVEN_COLS["ckv"][1]
    wkr = w_in[:, kr0:kr0 + MLA_ROPE]
    z64 = jnp.zeros((d, LANES - 2 * MLA_ROPE), F32)
    w = jnp.concatenate([w_in[:, :kr0], wkr, wkr, z64, _rot_cols(wkr), _rot_cols(wkr), z64], axis=1)
    uq = w_uq.reshape(MLA_Q_RANK, MLA_HEADS, MLA_NOPE + MLA_ROPE)
    ukv = w_ukv.reshape(MLA_KV_RANK, MLA_HEADS, MLA_NOPE + MLA_V)
    w_abs = jnp.einsum('rhd,lhd->rhl', uq[..., :MLA_NOPE], ukv[..., :MLA_NOPE],
                       precision=lax.Precision.HIGHEST)
    rope_w = uq[..., MLA_NOPE:]
    zq = jnp.zeros((MLA_Q_RANK, MLA_HEADS, MLA_QW - MLA_KV_RANK - 2 * MLA_ROPE), F32)
    wq = jnp.concatenate([w_abs, rope_w, _rot_cols(rope_w), zq], axis=-1)
    wq = wq.reshape(MLA_Q_RANK, MLA_HEADS * MLA_QW)
    vv = ukv[..., MLA_NOPE:].transpose(1, 0, 2)
    zv = jnp.zeros_like(vv)
    even = (jnp.arange(MLA_HEADS) % 2 == 0)[:, None, None]
    wuv = jnp.where(even, jnp.concatenate([vv, zv], axis=-1), jnp.concatenate([zv, vv], axis=-1))
    assert w.shape[1] == EVEN_N
    return w.astype(BF16), wq.astype(BF16), wuv.astype(BF16)


def _odd_weights(w_in):
    d = w_in.shape[0]
    src, _ = _layout(("q", DSA_HEADS * DSA_HD), ("k", DSA_KV_HEADS * DSA_HD), ("v", DSA_KV_HEADS * DSA_HD),
                     ("qi", IDX_HEADS * IDX_HD), ("ki", IDX_HD), ("wi", IDX_HEADS))
    part = lambda name: w_in[:, src[name][0]:src[name][1]]
    wq = part("q") * (DSA_HD ** -0.5)
    wk = part("k").reshape(d, DSA_KV_HEADS, DSA_HD)
    wv = part("v").reshape(d, DSA_KV_HEADS, DSA_HD)
    z = jnp.zeros((d, DSA_KV_HEADS, LANES - DSA_HD), F32)
    pad = lambda a: jnp.concatenate([a, z], -1).reshape(d, -1)
    wki = part("ki")
    z64 = jnp.zeros((d, LANES - IDX_HD), F32)
    w = jnp.concatenate([wq, pad(wk), pad(wv), part("qi"), wki, z64, z64, wki], axis=1)
    assert w.shape[1] == ODD_N
    wwt = jnp.concatenate([part("wi").T, jnp.zeros((2 * SUBLANES - IDX_HEADS, d), F32)], axis=0)
    return w.astype(BF16), wwt.astype(BF16)


def _block_diag(w):
    n, k, _ = w.shape
    eye = jnp.eye(n, dtype=w.dtype)
    return (w[:, :, None, :] * eye[:, None, :, None]).reshape(n * k, n * k)


def kernel(x, positions, e_norm, e_w_in, e_conv_w, e_conv_b, e_ga_w, e_ga_b, e_gx_w, e_gx_b, e_lambda,
           e_q_norm, e_w_uq, e_kv_norm, e_w_ukv, e_w_out, o_norm, o_w_in, o_idx_k_norm, o_w_out,
           m_norm, m_w1, m_w2, final_norm):
    b, s, d = x.shape
    m = b * s
    row = lambda v: v.reshape(1, -1).astype(F32)

    freqs = ROPE_THETA ** (-jnp.arange(0, MLA_ROPE, 2, dtype=F32) / MLA_ROPE)
    ang = positions.astype(F32)[..., None] * freqs
    cos2 = jnp.tile(jnp.cos(ang), (1, 1, 2)).reshape(m, MLA_ROPE)
    sin2 = jnp.tile(jnp.sin(ang), (1, 1, 2)).reshape(m, MLA_ROPE)
    z64 = jnp.zeros((m, LANES - 2 * MLA_ROPE), F32)
    qscale = LOG2E * (MLA_NOPE + MLA_ROPE) ** -0.5
    tq = qscale * jnp.concatenate([cos2, sin2, z64], axis=1)
    tc = jnp.concatenate([cos2, cos2, z64], axis=1)
    ts = jnp.concatenate([sin2, sin2, z64], axis=1)
    one_even = _one_hot_row(LANES, [MLA_ONE - LANES])
    vone_odd = _one_hot_row(DSA_KV_HEADS * LANES, [LANES * g + DSA_HD for g in range(DSA_KV_HEADS)])
    low = (jnp.arange(TK)[None, :] < jnp.arange(TK)[:, None]).astype(BF16)
    w1, w2 = m_w1.astype(BF16), m_w2.astype(BF16)
    e_wo, o_wo = e_w_out.astype(BF16), o_w_out.astype(BF16)

    x2 = x.reshape(m, d)
    for l in range(DEPTH):
        j = l // 2
        if l % 2 == 0:
            w, wq, wuv = _even_weights(e_w_in[j], e_w_uq[j], e_w_ukv[j])
            xr, gr, q, kv = _even_in(x2, row(e_norm[j]), w, row(e_q_norm[j]), wq, row(e_kv_norm[j]),
                                     tq, tc, ts, one_even, b, s, qscale)
            cl = -LRU_C * jax.nn.softplus(-e_lambda[j].astype(F32))
            y_lru = _lru(xr.reshape(b, s, -1), gr.reshape(b, s, -1), e_conv_w[j].astype(F32),
                         row(e_conv_b[j]), _block_diag(e_ga_w[j]).astype(BF16), row(e_ga_b[j]),
                         _block_diag(e_gx_w[j]).astype(BF16), row(e_gx_b[j]), row(cl))
            y_mla = _mla(q, kv.reshape(b, s, -1), wuv)
            mix = [(y_lru.reshape(m, -1), e_wo, (j, 0)), (y_mla.reshape(m, -1), e_wo, (j, 1))]
        else:
            w, wwt = _odd_weights(o_w_in[j])
            zpad = jnp.zeros((1, 2 * (LANES - IDX_HD)), F32)
            kig = jnp.concatenate([row(o_idx_k_norm[j]), zpad, row(o_idx_k_norm[j])], axis=1)
            q, k, v, qi, ki, wit = _odd_in(x2, row(o_norm[j]), w, wwt, kig, vone_odd, b, s)
            mix = [(_dsa(q, k, v, qi, ki, wit, low).reshape(m, -1), o_wo, (j, 0))]
        x2 = _mix_mlp(x2, mix, row(m_norm[l]), w1, w2, l, row(final_norm), final=(l == DEPTH - 1))
    return x2.reshape(b, s, d)
```

```python
import functools
import math

import jax
import jax.numpy as jnp
from jax import lax
from jax.experimental import pallas as pl
from jax.experimental.pallas import tpu as pltpu

F32 = jnp.float32
BF16 = jnp.bfloat16
I32 = jnp.int32

D_MODEL = 1024
DEPTH = 4
CHUNK = 64
EPS = 1e-6

LRU_WIDTH = D_MODEL // 2
LRU_BLOCKS = 8
CONV_W = 4
LRU_C = 8.0

MLA_HEADS = 8
MLA_NOPE = 64
MLA_ROPE = 32
MLA_V = 64
MLA_Q_RANK = D_MODEL // 4
MLA_KV_RANK = D_MODEL // 8
ROPE_THETA = 10000.0

DSA_HEADS = 16
DSA_KV_HEADS = 4
DSA_HD = 64
IDX_HEADS = 8
IDX_HD = 64
TOPK_MAX = 256

D_FF = 4 * D_MODEL

LANES = 128
SUBLANES = 8
VMEM_LIMIT = 56 * 1024 * 1024
LOG2E = 1.4426950408889634
NEG = -1e30

TM = 1024
TQ = 256
TK = 256
LRU_T = 256
MLP_TM = 1024
MLP_TF = 1024
NSLAB = 8


def _params(sem):
    return pltpu.CompilerParams(dimension_semantics=sem, vmem_limit_bytes=VMEM_LIMIT)


def _rms(x, g):
    ms = jnp.mean(x * x, axis=-1, keepdims=True)
    return (x * lax.rsqrt(ms + EPS)) * g


def _dot(a, b):
    return jnp.dot(a, b, preferred_element_type=F32)


def _dot_nt(a, b):
    return lax.dot_general(a, b, (((1,), (1,)), ((), ())), preferred_element_type=F32)


def _layout(*fields):
    cols, start = {}, 0
    for name, width in fields:
        cols[name] = (start, start + width)
        start += width
    return cols, start


def _slab_spec(nslab, rows, seq):
    nt = seq // rows
    return pl.BlockSpec((None, nslab, rows, LANES), lambda i: (i // nt, 0, i % nt, 0))


MLA_QW = 2 * LANES
MLA_ONE = LANES + 64
EVEN_COLS, EVEN_N = _layout(("xr", LRU_WIDTH), ("gr", LRU_WIDTH), ("cq", MLA_Q_RANK),
                            ("ckv", MLA_KV_RANK), ("kr", LANES), ("krs", LANES))


def _even_in_kernel(x_ref, g_ref, w_ref, qn_ref, wq_ref, kvn_ref, tq_ref, tc_ref, ts_ref, one_ref,
                    xr_ref, gr_ref, q_ref, kv_ref, *, qscale):
    h = _rms(x_ref[...], g_ref[...]).astype(BF16)
    cols = lambda name: slice(*EVEN_COLS[name])
    lat0 = EVEN_COLS["cq"][0]
    sub = lambda name: slice(EVEN_COLS[name][0] - lat0, EVEN_COLS[name][1] - lat0)
    xr_ref[...] = _dot(h, w_ref[:, cols("xr")])
    gr_ref[...] = _dot(h, w_ref[:, cols("gr")])
    lat = _dot(h, w_ref[:, lat0:EVEN_N])
    krope = lat[:, sub("kr")] * tc_ref[...] + lat[:, sub("krs")] * ts_ref[...]
    kv_ref[:, 0:LANES] = _rms(lat[:, sub("ckv")], kvn_ref[...]).astype(BF16)
    kv_ref[:, LANES:2 * LANES] = (krope + one_ref[...]).astype(BF16)
    cqn = _rms(lat[:, sub("cq")], qn_ref[...]).astype(BF16)
    tq = tq_ref[...]
    for c in range(MLA_HEADS // 2):
        qs = _dot(cqn, wq_ref[:, 2 * MLA_QW * c:2 * MLA_QW * (c + 1)])
        for e in range(2):
            o = MLA_QW * e
            q_ref[2 * c + e, :, 0:LANES] = (qs[:, o:o + LANES] * qscale).astype(BF16)
            q_ref[2 * c + e, :, LANES:MLA_QW] = (qs[:, o + LANES:o + MLA_QW] * tq).astype(BF16)


def _even_in(x2, g, w, qn, wq, kvn, tq, tc, ts, one, b, s, qscale):
    m = x2.shape[0]
    nt = s // TM
    row = lambda n: pl.BlockSpec((TM, n), lambda i: (i, 0))
    full = lambda a: pl.BlockSpec(a.shape, lambda i: (0, 0))
    sds = jax.ShapeDtypeStruct
    return pl.pallas_call(
        functools.partial(_even_in_kernel, qscale=qscale),
        grid=(m // TM,),
        in_specs=[row(D_MODEL), full(g), full(w), full(qn), full(wq), full(kvn),
                  row(LANES), row(LANES), row(LANES), full(one)],
        out_specs=[row(LRU_WIDTH), row(LRU_WIDTH),
                   pl.BlockSpec((None, MLA_HEADS, TM, MLA_QW), lambda i: (i // nt, 0, i % nt, 0)),
                   row(MLA_QW)],
        out_shape=[sds((m, LRU_WIDTH), F32), sds((m, LRU_WIDTH), F32),
                   sds((b, MLA_HEADS, s, MLA_QW), BF16),
                   sds((m, MLA_QW), BF16)],
        compiler_params=_params(("parallel",)),
        name="even_in",
    )(x2, g, w, qn, wq, kvn, tq, tc, ts, one)


def _gelu_tanh(x):
    c = math.sqrt(2.0 / math.pi)
    return 0.5 * x * (1.0 + jnp.tanh(c * (x + 0.044715 * (x * x * x))))


def _lru_kernel(xr_ref, gr_ref, cw_ref, cb_ref, wa_ref, ba_ref, wx_ref, bx_ref, cl_ref, y_ref,
                xbuf, hc, a_scr, b_scr):
    t = LRU_T

    @pl.when(pl.program_id(1) == 0)
    def _():
        xbuf[0:8, :] = jnp.zeros((8, LRU_WIDTH), F32)
        hc[...] = jnp.zeros((1, LRU_WIDTH), F32)

    x = xr_ref[...]
    xbuf[8:8 + t, :] = x
    xc = cb_ref[...] + cw_ref[3:4, :] * x
    for j in range(CONV_W - 1):
        xc = xc + cw_ref[j:j + 1, :] * xbuf[5 + j:5 + j + t, :]
    xbuf[0:8, :] = x[t - 8:t, :]

    xcb = xc.astype(BF16)
    sigmoid = lambda z: 0.5 * jnp.tanh(0.5 * z) + 0.5
    r = sigmoid(_dot(xcb, wa_ref[...]) + ba_ref[...])
    i = sigmoid(_dot(xcb, wx_ref[...]) + bx_ref[...])
    log_a = cl_ref[...] * r
    a = jnp.exp(log_a)
    th = jnp.tanh(log_a)
    b = jnp.sqrt((-2.0 * th) / (1.0 - th)) * (i * xc)

    def doubling(a, b, axis):
        n = a.shape[axis]
        pos = lax.broadcasted_iota(I32, a.shape, axis)
        d = 1
        while d < n:
            keep = pos >= d
            b = jnp.where(keep, a * pltpu.roll(b, d, axis) + b, b)
            a = jnp.where(keep, a * pltpu.roll(a, d, axis), a)
            d *= 2
        return a, b

    ng = t // SUBLANES
    a, b = doubling(a.reshape(ng, SUBLANES, LRU_WIDTH), b.reshape(ng, SUBLANES, LRU_WIDTH), 1)
    a = a.reshape(t, LRU_WIDTH)
    b = b.reshape(t, LRU_WIDTH)
    nl = LRU_WIDTH // LANES
    for j in range(nl):
        a_scr[j] = a[:, LANES * j:LANES * (j + 1)]
        b_scr[j] = b[:, LANES * j:LANES * (j + 1)]
    ends = pl.ds(SUBLANES - 1, ng, stride=SUBLANES)
    grp = lax.broadcasted_iota(I32, (ng, LRU_WIDTH), 0)
    ae, be = doubling(jnp.concatenate([a_scr[j, ends, :] for j in range(nl)], axis=1),
                      jnp.concatenate([b_scr[j, ends, :] for j in range(nl)], axis=1), 0)
    h_end = ae * hc[...] + be
    h_in = jnp.where(grp >= 1, pltpu.roll(h_end, 1, 0), hc[...])
    hc[...] = h_end[ng - 1:ng, :]
    gate = _gelu_tanh(gr_ref[...])
    for g in range(0, ng, 2):
        hs = []
        for gg in (g, g + 1):
            rows = slice(SUBLANES * gg, SUBLANES * (gg + 1))
            hs.append(a[rows, :] * h_in[gg:gg + 1, :] + b[rows, :])
        rows = slice(SUBLANES * g, SUBLANES * (g + 2))
        y_ref[rows, :] = (jnp.concatenate(hs, axis=0) * gate[rows, :]).astype(BF16)


def _lru(xr, gr, cw, cb, wa, ba, wx, bx, cl):
    b, s, w = xr.shape
    blk = pl.BlockSpec((None, LRU_T, w), lambda i, j: (i, j, 0))
    full = lambda a: pl.BlockSpec(a.shape, lambda i, j: (0, 0))
    return pl.pallas_call(
        _lru_kernel,
        grid=(b, s // LRU_T),
        in_specs=[blk, blk, full(cw), full(cb), full(wa), full(ba), full(wx), full(bx), full(cl)],
        out_specs=blk,
        out_shape=jax.ShapeDtypeStruct((b, s, w), BF16),
        scratch_shapes=[pltpu.VMEM((LRU_T + 8, w), F32), pltpu.VMEM((1, w), F32),
                        pltpu.VMEM((w // LANES, LRU_T, LANES), F32),
                        pltpu.VMEM((w // LANES, LRU_T, LANES), F32)],
        compiler_params=_params(("parallel", "arbitrary")),
        name="rg_lru",
    )(xr, gr, cw, cb, wa, ba, wx, bx, cl)


def _for_tiles(n, body, init):
    def run(start, count, carry):
        for j in range(count):
            carry = body(start + j, carry)
        return carry

    carry = lax.fori_loop(0, n // 4, lambda kq, cr: run(4 * kq, 4, cr), init)
    done = (n // 4) * 4
    carry = lax.cond(n - done >= 2, lambda cr: run(done, 2, cr), lambda cr: cr, carry)
    return lax.cond(n % 2 == 1, lambda cr: run(n - 1, 1, cr), lambda cr: cr, carry)


def _key_rows(kb):
    return pl.ds(pl.multiple_of(kb * TK, TK), TK)


def _chunk_of(pos):
    return pos >> (CHUNK.bit_length() - 1)


def _admissible(c, kb):
    rc = _chunk_of(lax.broadcasted_iota(I32, (TQ, 1), 0) + c * TQ)
    cc = _chunk_of(lax.broadcasted_iota(I32, (1, TK), 1) + kb * TK)
    return cc <= rc


def _admissible_t(c, kb):
    kc = _chunk_of(lax.broadcasted_iota(I32, (TK, 1), 0) + kb * TK)
    qc = _chunk_of(lax.broadcasted_iota(I32, (1, TQ), 1) + c * TQ)
    return kc <= qc


def _flash_reset(m_scr, acc_scr):
    m_scr[...] = jnp.full(m_scr.shape, -jnp.inf, F32)
    acc_scr[...] = jnp.zeros(acc_scr.shape, F32)


def _flash_step(s, v, m_ref, acc_ref):
    m_old = m_ref[...]
    m_new = jnp.maximum(m_old, jnp.max(s, axis=1, keepdims=True))
    p = jnp.exp2(s - jnp.tile(m_new, (1, s.shape[1] // LANES))).astype(BF16)
    alpha = jnp.tile(jnp.exp2(m_old - m_new), (1, acc_ref.shape[1] // LANES))
    acc_ref[...] = alpha * acc_ref[...] + _dot(p, v)
    m_ref[...] = m_new


MLA_CHAINS = 1


def _mla_kernel(q_ref, kv_ref, wuv_ref, o_ref, m_scr, acc_scr):
    c = pl.program_id(1)
    kd = (c * TQ) // TK
    per = MLA_HEADS // MLA_CHAINS
    _flash_reset(m_scr, acc_scr)

    def tile(kb, bias):
        kv = kv_ref[_key_rows(kb), :]
        for ch in range(MLA_CHAINS):
            s = _dot_nt(q_ref[per * ch:per * (ch + 1)].reshape(per * TQ, MLA_QW), kv)
            if bias is not None:
                s = s + bias
            _flash_step(s, kv, m_scr.at[ch], acc_scr.at[ch])

    def past_tile(kb, carry):
        tile(kb, None)
        return carry

    _for_tiles(kd, past_tile, 0)
    dbias = jnp.where(_admissible(c, kd), 0.0, NEG).astype(F32)
    tile(kd, jnp.concatenate([dbias] * per, axis=0))

    def latent(hd):
        rows = slice((hd % per) * TQ, (hd % per + 1) * TQ)
        acc = acc_scr[hd // per, rows, :]
        return (acc[:, 0:LANES] * (1.0 / acc[:, MLA_ONE:MLA_ONE + 1])).astype(BF16)

    for p in range(MLA_HEADS // 2):
        y = _dot(latent(2 * p), wuv_ref[2 * p]) + _dot(latent(2 * p + 1), wuv_ref[2 * p + 1])
        o_ref[:, LANES * p:LANES * (p + 1)] = y.astype(BF16)


def _mla(q, kv, wuv):
    b, _, s, _ = q.shape
    assert TQ == TK
    per = MLA_HEADS // MLA_CHAINS
    return pl.pallas_call(
        _mla_kernel,
        grid=(b, s // TQ),
        in_specs=[pl.BlockSpec((None, MLA_HEADS, TQ, MLA_QW), lambda i, j: (i, 0, j, 0)),
                  pl.BlockSpec((None, s, MLA_QW), lambda i, j: (i, 0, 0)),
                  pl.BlockSpec(wuv.shape, lambda i, j: (0, 0, 0))],
        out_specs=pl.BlockSpec((None, TQ, MLA_HEADS * MLA_V), lambda i, j: (i, j, 0)),
        out_shape=jax.ShapeDtypeStruct((b, s, MLA_HEADS * MLA_V), BF16),
        scratch_shapes=[pltpu.VMEM((MLA_CHAINS, per * TQ, LANES), F32),
                        pltpu.VMEM((MLA_CHAINS, per * TQ, MLA_QW), F32)],
        compiler_params=_params(("parallel", "arbitrary")),
        name="mla_attn",
    )(q, kv, wuv)


ODD_COLS, ODD_N = _layout(("q", DSA_HEADS * DSA_HD), ("k", DSA_KV_HEADS * LANES), ("v", DSA_KV_HEADS * LANES),
                          ("qi", IDX_HEADS * IDX_HD), ("ki", 2 * LANES))


def _odd_in_kernel(x_ref, g_ref, w_ref, wwt_ref, kig_ref, vone_ref,
                   q_ref, k_ref, v_ref, qi_ref, ki_ref, wit_ref):
    h = _rms(x_ref[...], g_ref[...]).astype(BF16)
    cols = lambda name: slice(*ODD_COLS[name])
    lane_slab = lambda a, j: a[:, LANES * j:LANES * (j + 1)]
    half = ODD_COLS["q"][1] // 2
    for c in range(2):
        qc = _dot(h, w_ref[:, half * c:half * (c + 1)]) * LOG2E
        for j in range(NSLAB // 2):
            q_ref[NSLAB // 2 * c + j] = lane_slab(qc, j).astype(BF16)
    kc = _dot(h, w_ref[:, cols("k")])
    vc = _dot(h, w_ref[:, cols("v")]) + vone_ref[...]
    ic = _dot(h, w_ref[:, cols("qi")])
    for j in range(DSA_KV_HEADS):
        k_ref[j] = lane_slab(kc, j).astype(BF16)
        v_ref[j] = lane_slab(vc, j).astype(BF16)
    for j in range(IDX_HEADS // 2):
        qi_ref[j] = lane_slab(ic, j).astype(BF16)
    kiab = _dot(h, w_ref[:, cols("ki")])
    kia = kiab[:, 0:LANES]
    kib = kiab[:, LANES:2 * LANES]
    inv = lax.rsqrt(jnp.sum(kia * kia, axis=-1, keepdims=True) * (1.0 / IDX_HD) + EPS)
    ki_ref[0] = ((kia * inv) * kig_ref[:, 0:LANES]).astype(BF16)
    ki_ref[1] = ((kib * inv) * kig_ref[:, LANES:2 * LANES]).astype(BF16)
    wit = _dot_nt(wwt_ref[...], h)[0:IDX_HEADS, :]
    wit_ref[...] = (wit * (IDX_HEADS ** -0.5)) * (IDX_HD ** -0.5)


def _odd_in(x2, g, w, wwt, kig, vone, b, s):
    m = x2.shape[0]
    row = lambda n: pl.BlockSpec((TM, n), lambda i: (i, 0))
    full = lambda a: pl.BlockSpec(a.shape, lambda i: (0, 0))
    sds = jax.ShapeDtypeStruct
    slab = lambda n: sds((b, n, s, LANES), BF16)
    return pl.pallas_call(
        _odd_in_kernel,
        grid=(m // TM,),
        in_specs=[row(D_MODEL), full(g), full(w), full(wwt), full(kig), full(vone)],
        out_specs=[_slab_spec(NSLAB, TM, s), _slab_spec(4, TM, s), _slab_spec(4, TM, s),
                   _slab_spec(4, TM, s), _slab_spec(2, TM, s),
                   pl.BlockSpec((IDX_HEADS, TM), lambda i: (0, i))],
        out_shape=[slab(NSLAB), slab(4), slab(4), slab(4), slab(2), sds((IDX_HEADS, m), F32)],
        compiler_params=_params(("parallel",)),
        name="odd_in",
    )(x2, g, w, wwt, kig, vone)


KEY_NEG_INF = -2 ** 31 + 0x7FFFFF


def _key_to_score(key):
    return pltpu.bitcast(jnp.where(key < 0, key ^ jnp.int32(0x7FFFFFFF), key), F32)


def _dsa_kernel(q_ref, k_ref, v_ref, qi_ref, ki_ref, wit_ref, low_ref, o_ref,
                key_scr, bias_scr, q4_scr, m_scr, acc_scr, *, topk):
    c = pl.program_id(1)
    ntile = (c * TQ) // TK + 1
    part = 4 * SUBLANES

    def score_tile(kb, carry):
        rows = _key_rows(kb)
        score = jnp.zeros((TK, TQ), F32)
        for p in range(IDX_HEADS // 2):
            slab = qi_ref[p]
            for e in range(2):
                hd = 2 * p + e
                z = jnp.maximum(_dot_nt(ki_ref[e, rows, :], slab), 0.0)
                score = score + wit_ref[hd:hd + 1, :] * z
        score = jnp.where(_admissible_t(c, kb), score, -jnp.inf)
        key_scr[kb] = jnp.where(score == 0.0, 0.0, score)
        return carry

    _for_tiles(ntile, score_tile, 0)

    def count(pred):
        def tile(kb, acc):
            hit = jnp.where(pred(key_scr[kb]), 1.0, 0.0)
            return acc + hit.reshape(TK // part, part, TQ).sum(axis=0)
        acc = _for_tiles(ntile, tile, jnp.zeros((part, TQ), F32))
        return jnp.sum(acc, axis=0, keepdims=True)

    total = (ntile * TK).astype(F32)

    def step(i, tkey):
        cand = tkey + jnp.left_shift(jnp.int32(1), 31 - i)
        cand_f = _key_to_score(cand)
        cnt = count(lambda sc: sc >= cand_f)
        cnt = jnp.where(cand <= KEY_NEG_INF, total, cnt)
        return jnp.where(cnt >= float(topk), cand, tkey)

    nbits = jnp.where(ntile * TK <= topk, 0, 32)
    tkey = lax.fori_loop(0, nbits, step, jnp.full((1, TQ), -2 ** 31, I32))
    t = _key_to_score(jnp.maximum(tkey, KEY_NEG_INF))

    need = float(topk) - count(lambda sc: sc > t)

    def bias_tile(kb, ties_before):
        key = key_scr[kb]
        eq = jnp.where(key == t, 1.0, 0.0)
        prefix = _dot(low_ref[...], eq.astype(BF16)) + ties_before
        sel = jnp.where(key > t, 1.0, jnp.where(prefix < need, eq, 0.0))
        sel = jnp.where(_admissible_t(c, kb), sel, 0.0)
        bias_scr[kb] = jnp.where(sel > 0.0, 0.0, NEG).T
        return ties_before + jnp.sum(eq, axis=0, keepdims=True)

    _for_tiles(ntile, bias_tile, jnp.zeros((1, TQ), F32))

    rep = DSA_HEADS // DSA_KV_HEADS
    lane = lax.broadcasted_iota(I32, (TQ, LANES), 1)
    low_half = lane < DSA_HD
    for gp in range(NSLAB):
        g, r0 = gp // 2, 2 * (gp % 2)
        slab = q_ref[gp].astype(F32)
        q4_scr[g, r0 * TQ:(r0 + 1) * TQ, :] = jnp.where(low_half, slab, 0.0).astype(BF16)
        q4_scr[g, (r0 + 1) * TQ:(r0 + 2) * TQ, :] = jnp.where(
            low_half, pltpu.roll(slab, DSA_HD, 1), 0.0).astype(BF16)
    _flash_reset(m_scr, acc_scr)

    def attn_tile(kb, carry):
        rows = _key_rows(kb)
        bias = bias_scr[kb]
        bias4 = jnp.concatenate([bias] * rep, axis=0)
        for g in range(DSA_KV_HEADS):
            s = _dot_nt(q4_scr[g], k_ref[g, rows, :]) + bias4
            _flash_step(s, v_ref[g, rows, :], m_scr.at[g], acc_scr.at[g])
        return carry

    _for_tiles(ntile, attn_tile, 0)

    for gp in range(NSLAB):
        g, r0 = gp // 2, 2 * (gp % 2)
        acc_e = acc_scr[g, r0 * TQ:(r0 + 1) * TQ, :]
        acc_o = acc_scr[g, (r0 + 1) * TQ:(r0 + 2) * TQ, :]
        oe = acc_e * (1.0 / acc_e[:, DSA_HD:DSA_HD + 1])
        oo = pltpu.roll(acc_o * (1.0 / acc_o[:, DSA_HD:DSA_HD + 1]), DSA_HD, 1)
        o_ref[:, LANES * gp:LANES * (gp + 1)] = jnp.where(low_half, oe, oo).astype(BF16)


def _dsa(q, k, v, qi, ki, wit, low):
    b, _, s, _ = q.shape
    topk = min(TOPK_MAX, s // 4)
    assert topk <= TK and s % TK == 0 and TK % TQ == 0
    qblk = lambda n: pl.BlockSpec((None, n, TQ, LANES), lambda i, j: (i, 0, j, 0))
    kblk = lambda n: pl.BlockSpec((None, n, s, LANES), lambda i, j: (i, 0, 0, 0))
    nkb, nq = s // TK, s // TQ
    rep = DSA_HEADS // DSA_KV_HEADS
    return pl.pallas_call(
        functools.partial(_dsa_kernel, topk=topk),
        grid=(b, nq),
        in_specs=[qblk(NSLAB), kblk(DSA_KV_HEADS), kblk(DSA_KV_HEADS), qblk(4), kblk(2),
                  pl.BlockSpec((IDX_HEADS, TQ), lambda i, j: (0, i * nq + j)),
                  pl.BlockSpec((TK, TK), lambda i, j: (0, 0))],
        out_specs=pl.BlockSpec((None, TQ, DSA_HEADS * DSA_HD), lambda i, j: (i, j, 0)),
        out_shape=jax.ShapeDtypeStruct((b, s, DSA_HEADS * DSA_HD), BF16),
        scratch_shapes=[pltpu.VMEM((nkb, TK, TQ), F32), pltpu.VMEM((nkb, TQ, TK), F32),
                        pltpu.VMEM((DSA_KV_HEADS, rep * TQ, LANES), BF16),
                        pltpu.VMEM((DSA_KV_HEADS, rep * TQ, LANES), F32),
                        pltpu.VMEM((DSA_KV_HEADS, rep * TQ, LANES), F32)],
        compiler_params=_params(("parallel", "arbitrary")),
        name="dsa_attn",
    )(q, k, v, qi, ki, wit, low)


def _mix_mlp_kernel(*refs, n_mix, final):
    x_ref = refs[0]
    mix = refs[1:1 + 2 * n_mix]
    g_ref, w1_ref, w2_ref, fg_ref, o_ref, x1_scr, h_scr, acc = refs[1 + 2 * n_mix:]
    j = pl.program_id(1)

    @pl.when(j == 0)
    def _():
        x1 = x_ref[...]
        for a_ref, w_ref in zip(mix[0::2], mix[1::2]):
            x1 = x1 + _dot(a_ref[...], w_ref[...])
        x1_scr[...] = x1
        h_scr[...] = _rms(x1, g_ref[...]).astype(BF16)
        acc[...] = jnp.zeros_like(acc)

    a = jnp.maximum(_dot(h_scr[...], w1_ref[...]), 0.0)
    acc[...] += _dot((a * a).astype(BF16), w2_ref[...])

    @pl.when(j == pl.num_programs(1) - 1)
    def _():
        y = x1_scr[...] + acc[...]
        o_ref[...] = _rms(y, fg_ref[...]) if final else y


def _mix_mlp(x2, mix, g, w1, w2, layer, fg, final):
    m = x2.shape[0]
    specs, args = [pl.BlockSpec((MLP_TM, D_MODEL), lambda i, j: (i, 0))], [x2]
    for a, w, (n, rb) in mix:
        k = a.shape[1]
        specs += [pl.BlockSpec((MLP_TM, k), lambda i, j: (i, 0)),
                  pl.BlockSpec((None, k, D_MODEL), lambda i, j, n=n, rb=rb: (n, rb, 0))]
        args += [a, w]
    specs += [pl.BlockSpec((1, D_MODEL), lambda i, j: (0, 0)),
              pl.BlockSpec((None, D_MODEL, MLP_TF), lambda i, j: (layer, 0, j)),
              pl.BlockSpec((None, MLP_TF, D_MODEL), lambda i, j: (layer, j, 0)),
              pl.BlockSpec((1, D_MODEL), lambda i, j: (0, 0))]
    args += [g, w1, w2, fg]
    return pl.pallas_call(
        functools.partial(_mix_mlp_kernel, n_mix=len(mix), final=final),
        grid=(m // MLP_TM, D_FF // MLP_TF),
        in_specs=specs,
        out_specs=pl.BlockSpec((MLP_TM, D_MODEL), lambda i, j: (i, 0)),
        out_shape=jax.ShapeDtypeStruct((m, D_MODEL), F32),
        scratch_shapes=[pltpu.VMEM((MLP_TM, D_MODEL), F32), pltpu.VMEM((MLP_TM, D_MODEL), BF16),
                        pltpu.VMEM((MLP_TM, D_MODEL), F32)],
        compiler_params=_params(("parallel", "arbitrary")),
        name="mix_mlp",
    )(*args)


def _rot_cols(w):
    half = w.shape[-1] // 2
    return jnp.concatenate([-w[..., half:], w[..., :half]], axis=-1)


def _one_hot_row(n, idx):
    return jnp.zeros((1, n), F32).at[0, jnp.asarray(idx)].set(1.0)


def _even_weights(w_in, w_uq, w_ukv):
    d = w_in.shape[0]
    kr0 = EVEN_COLS["ckv"][1]
    wkr = w_in[:, kr0:kr0 + MLA_ROPE]
    z64 = jnp.zeros((d, LANES - 2 * MLA_ROPE), F32)
    w = jnp.concatenate([w_in[:, :kr0], wkr, wkr, z64, _rot_cols(wkr), _rot_cols(wkr), z64], axis=1)
    uq = w_uq.reshape(MLA_Q_RANK, MLA_HEADS, MLA_NOPE + MLA_ROPE)
    ukv = w_ukv.reshape(MLA_KV_RANK, MLA_HEADS, MLA_NOPE + MLA_V)
    w_abs = jnp.einsum('rhd,lhd->rhl', uq[..., :MLA_NOPE], ukv[..., :MLA_NOPE],
                       precision=lax.Precision.HIGHEST)
    rope_w = uq[..., MLA_NOPE:]
    zq = jnp.zeros((MLA_Q_RANK, MLA_HEADS, MLA_QW - MLA_KV_RANK - 2 * MLA_ROPE), F32)
    wq = jnp.concatenate([w_abs, rope_w, _rot_cols(rope_w), zq], axis=-1)
    wq = wq.reshape(MLA_Q_RANK, MLA_HEADS * MLA_QW)
    vv = ukv[..., MLA_NOPE:].transpose(1, 0, 2)
    zv = jnp.zeros_like(vv)
    even = (jnp.arange(MLA_HEADS) % 2 == 0)[:, None, None]
    wuv = jnp.where(even, jnp.concatenate([vv, zv], axis=-1), jnp.concatenate([zv, vv], axis=-1))
    assert w.shape[1] == EVEN_N
    return w.astype(BF16), wq.astype(BF16), wuv.astype(BF16)


def _odd_weights(w_in):
    d = w_in.shape[0]
    src, _ = _layout(("q", DSA_HEADS * DSA_HD), ("k", DSA_KV_HEADS * DSA_HD), ("v", DSA_KV_HEADS * DSA_HD),
                     ("qi", IDX_HEADS * IDX_HD), ("ki", IDX_HD), ("wi", IDX_HEADS))
    part = lambda name: w_in[:, src[name][0]:src[name][1]]
    wq = part("q") * (DSA_HD ** -0.5)
    wk = part("k").reshape(d, DSA_KV_HEADS, DSA_HD)
    wv = part("v").reshape(d, DSA_KV_HEADS, DSA_HD)
    z = jnp.zeros((d, DSA_KV_HEADS, LANES - DSA_HD), F32)
    pad = lambda a: jnp.concatenate([a, z], -1).reshape(d, -1)
    wki = part("ki")
    z64 = jnp.zeros((d, LANES - IDX_HD), F32)
    w = jnp.concatenate([wq, pad(wk), pad(wv), part("qi"), wki, z64, z64, wki], axis=1)
    assert w.shape[1] == ODD_N
    wwt = jnp.concatenate([part("wi").T, jnp.zeros((2 * SUBLANES - IDX_HEADS, d), F32)], axis=0)
    return w.astype(BF16), wwt.astype(BF16)


def _block_diag(w):
    n, k, _ = w.shape
    eye = jnp.eye(n, dtype=w.dtype)
    return (w[:, :, None, :] * eye[:, None, :, None]).reshape(n * k, n * k)


def kernel(x, positions, e_norm, e_w_in, e_conv_w, e_conv_b, e_ga_w, e_ga_b, e_gx_w, e_gx_b, e_lambda,
           e_q_norm, e_w_uq, e_kv_norm, e_w_ukv, e_w_out, o_norm, o_w_in, o_idx_k_norm, o_w_out,
           m_norm, m_w1, m_w2, final_norm):
    b, s, d = x.shape
    m = b * s
    row = lambda v: v.reshape(1, -1).astype(F32)

    freqs = ROPE_THETA ** (-jnp.arange(0, MLA_ROPE, 2, dtype=F32) / MLA_ROPE)
    ang = positions.astype(F32)[..., None] * freqs
    cos2 = jnp.tile(jnp.cos(ang), (1, 1, 2)).reshape(m, MLA_ROPE)
    sin2 = jnp.tile(jnp.sin(ang), (1, 1, 2)).reshape(m, MLA_ROPE)
    z64 = jnp.zeros((m, LANES - 2 * MLA_ROPE), F32)
    qscale = LOG2E * (MLA_NOPE + MLA_ROPE) ** -0.5
    tq = qscale * jnp.concatenate([cos2, sin2, z64], axis=1)
    tc = jnp.concatenate([cos2, cos2, z64], axis=1)
    ts = jnp.concatenate([sin2, sin2, z64], axis=1)
    one_even = _one_hot_row(LANES, [MLA_ONE - LANES])
    vone_odd = _one_hot_row(DSA_KV_HEADS * LANES, [LANES * g + DSA_HD for g in range(DSA_KV_HEADS)])
    low = (jnp.arange(TK)[None, :] < jnp.arange(TK)[:, None]).astype(BF16)
    w1, w2 = m_w1.astype(BF16), m_w2.astype(BF16)
    e_wo, o_wo = e_w_out.astype(BF16), o_w_out.astype(BF16)

    x2 = x.reshape(m, d)
    for l in range(DEPTH):
        j = l // 2
        if l % 2 == 0:
            w, wq, wuv = _even_weights(e_w_in[j], e_w_uq[j], e_w_ukv[j])
            xr, gr, q, kv = _even_in(x2, row(e_norm[j]), w, row(e_q_norm[j]), wq, row(e_kv_norm[j]),
                                     tq, tc, ts, one_even, b, s, qscale)
            cl = -LRU_C * jax.nn.softplus(-e_lambda[j].astype(F32))
            y_lru = _lru(xr.reshape(b, s, -1), gr.reshape(b, s, -1), e_conv_w[j].astype(F32),
                         row(e_conv_b[j]), _block_diag(e_ga_w[j]).astype(BF16), row(e_ga_b[j]),
                         _block_diag(e_gx_w[j]).astype(BF16), row(e_gx_b[j]), row(cl))
            y_mla = _mla(q, kv.reshape(b, s, -1), wuv)
            mix = [(y_lru.reshape(m, -1), e_wo, (j, 0)), (y_mla.reshape(m, -1), e_wo, (j, 1))]
        else:
            w, wwt = _odd_weights(o_w_in[j])
            zpad = jnp.zeros((1, 2 * (LANES - IDX_HD)), F32)
            kig = jnp.concatenate([row(o_idx_k_norm[j]), zpad, row(o_idx_k_norm[j])], axis=1)
            q, k, v, qi, ki, wit = _odd_in(x2, row(o_norm[j]), w, wwt, kig, vone_odd, b, s)
            mix = [(_dsa(q, k, v, qi, ki, wit, low).reshape(m, -1), o_wo, (j, 0))]
        x2 = _mix_mlp(x2, mix, row(m_norm[l]), w1, w2, l, row(final_norm), final=(l == DEPTH - 1))
    return x2.reshape(b, s, d)
```
